```python
import math
import jax, jax.numpy as jnp
from jax import lax
import numpy as np

D_MODEL = 4096
BATCH = 4
SEQ = 2048
DEPTH = 1
DEC_BATCH = 128
DEC_SEQ = 8
PAST_LEN = 16384
PAGE_SIZE = 128

SSD_EXPAND = 2
D_INNER = SSD_EXPAND * D_MODEL
HEAD_DIM = 64
N_SSD_HEADS = D_INNER // HEAD_DIM
N_BC_GROUPS = 8
D_STATE = 128
CONV_WIDTH = 4
CHUNK = 128
CONV_DIM = D_INNER + 2 * N_BC_GROUPS * D_STATE
D_POOL = D_MODEL
POOL_WINDOWS = (2, 4, 8, 16)
N_POOL_GROUPS = len(POOL_WINDOWS)
POOL_GROUP = D_POOL // N_POOL_GROUPS
POOL_KEEP = max(POOL_WINDOWS) - 1
D_FF = 4 * D_MODEL
D_PLE = 256
EPS = 1e-6
SSD_NORM_EPS = 1e-5
IN_SPLITS = (D_INNER, D_INNER + CONV_DIM, D_INNER + CONV_DIM + N_SSD_HEADS,
             D_INNER + CONV_DIM + N_SSD_HEADS + D_POOL,
             D_INNER + CONV_DIM + N_SSD_HEADS + D_POOL + D_MODEL)
IN_COLS = IN_SPLITS[-1] + D_MODEL

kernel_name = 'ssd_pool_gated_hybrid_step'


def rms_norm(x, g, eps=EPS):
    xf = x.astype(jnp.float32)
    y = xf * lax.rsqrt(jnp.mean(xf * xf, axis=-1, keepdims=True) + eps)
    return (y * g.astype(jnp.float32)).astype(x.dtype)


def causal_dwconv(u, prev, w, b):
    full = jnp.concatenate([prev.astype(u.dtype), u], axis=1)
    y = lax.conv_general_dilated(full, w[:, None, :].astype(u.dtype), window_strides=(1,), padding='VALID',
                                 dimension_numbers=('NWC', 'WIO', 'NWC'), feature_group_count=u.shape[-1])
    return y + b.astype(u.dtype), full[:, -(CONV_WIDTH - 1):]


def multiscale_pool(u, prev, n_past):
    L = u.shape[1]
    full = jnp.concatenate([prev.astype(u.dtype), u], axis=1).astype(jnp.float32)
    csum = jnp.pad(jnp.cumsum(full, axis=1), ((0, 0), (1, 0), (0, 0)))
    end = csum[:, POOL_KEEP + 1:]
    t = jnp.arange(L)
    outs = []
    for gi, w in enumerate(POOL_WINDOWS):
        lo, hi = gi * POOL_GROUP, (gi + 1) * POOL_GROUP
        start = csum[:, POOL_KEEP + 1 - w:POOL_KEEP + 1 - w + L, lo:hi]
        cnt = jnp.minimum(t + n_past + 1, w).astype(jnp.float32)
        outs.append((end[..., lo:hi] - start) / cnt[None, :, None])
    mean = jnp.concatenate(outs, axis=-1)
    return (mean - full[:, POOL_KEEP:]).astype(u.dtype), full[:, -POOL_KEEP:].astype(u.dtype)


def ssd_scan(x, dt, A, B, C, s0):
    b, l, h, p = x.shape
    g, n = B.shape[2], B.shape[3]
    hg = h // g
    q = math.gcd(l, CHUNK)
    c = l // q
    xc = x.reshape(b, c, q, g, hg, p)
    dtc = dt.reshape(b, c, q, g, hg)
    Bc = B.reshape(b, c, q, g, n)
    Cc = C.reshape(b, c, q, g, n)
    acum = jnp.cumsum(dtc * A.reshape(g, hg), axis=2)
    xdt = xc * dtc[..., None]
    acum_t = jnp.moveaxis(acum, 2, -1)
    seg = acum_t[..., :, None] - acum_t[..., None, :]
    mask = jnp.tril(jnp.ones((q, q), dtype=bool))
    lmat = jnp.exp(jnp.where(mask, seg, -jnp.inf))
    cb = jnp.einsum('bcign,bcjgn->bcgij', Cc, Bc)
    y_diag = jnp.einsum('bcgij,bcghij,bcjghp->bcighp', cb, lmat, xdt)
    decay = jnp.exp(acum[:, :, -1:] - acum)
    states = jnp.einsum('bcjgn,bcjgh,bcjghp->bcghpn', Bc, decay, xdt)
    chunk_decay = jnp.exp(acum[:, :, -1])

    def step(s, inp):
        st, dec = inp
        return s * dec[..., None, None] + st, s

    s_final, s_prev = lax.scan(step, s0.reshape(b, g, hg, p, n),
                               (jnp.moveaxis(states, 1, 0), jnp.moveaxis(chunk_decay, 1, 0)))
    s_prev = jnp.moveaxis(s_prev, 0, 1)
    y_off = jnp.einsum('bcign,bcghpn,bcigh->bcighp', Cc, s_prev, jnp.exp(acum))
    y = (y_diag + y_off).reshape(b, l, h, p)
    return y, s_final.reshape(b, h, p, n)


def decoder_layer(x, p_emb, ssm_prev, conv_prev, pool_prev, n_past,
                  g_mix, w_in, conv_w, conv_b, dt_bias, a_log, d_skip, g_ssd_out, w_ssd_out,
                  w_pool_mix, pool_scale, w_pool_out, w_o, g_ffn, w_up, w_down, g_ple, w_ple_gate, w_ple_proj):
    bsz, L, _ = x.shape
    h = rms_norm(x, g_mix)
    proj = h @ w_in
    z, xbc, dt_raw, u, ga, gb = jnp.split(proj, IN_SPLITS, axis=-1)
    xbc, conv_new = causal_dwconv(xbc, conv_prev, conv_w, conv_b)
    xbc = jax.nn.silu(xbc).astype(jnp.float32)
    xs = xbc[..., :D_INNER].reshape(bsz, L, N_SSD_HEADS, HEAD_DIM)
    bm = xbc[..., D_INNER:D_INNER + N_BC_GROUPS * D_STATE].reshape(bsz, L, N_BC_GROUPS, D_STATE)
    cm = xbc[..., D_INNER + N_BC_GROUPS * D_STATE:].reshape(bsz, L, N_BC_GROUPS, D_STATE)
    dt = jax.nn.softplus(dt_raw.astype(jnp.float32) + dt_bias.astype(jnp.float32))
    A = -jnp.exp(a_log.astype(jnp.float32))
    y, ssm_new = ssd_scan(xs, dt, A, bm, cm, ssm_prev.astype(jnp.float32))
    y = (y + d_skip.astype(jnp.float32)[:, None] * xs).reshape(bsz, L, D_INNER)
    y = y * jax.nn.silu(z.astype(jnp.float32))
    yg = y.reshape(bsz, L, N_BC_GROUPS, D_INNER // N_BC_GROUPS)
    yg = yg * lax.rsqrt(jnp.mean(yg * yg, axis=-1, keepdims=True) + SSD_NORM_EPS)
    y = yg.reshape(bsz, L, D_INNER) * g_ssd_out.astype(jnp.float32)
    ya = y.astype(x.dtype) @ w_ssd_out
    pooled, pool_new = multiscale_pool(u, pool_prev, n_past)
    pooled = jnp.einsum('blgc,gcd->blgd', pooled.reshape(bsz, L, N_POOL_GROUPS, POOL_GROUP), w_pool_mix)
    pooled = pooled.reshape(bsz, L, D_POOL) * pool_scale
    yb = pooled @ w_pool_out
    merged = jax.nn.sigmoid(ga) * ya + jax.nn.sigmoid(gb) * yb
    x = x + merged @ w_o
    h2 = rms_norm(x, g_ffn)
    x = x + jnp.square(jax.nn.relu(h2 @ w_up)) @ w_down
    gate = jax.nn.sigmoid(rms_norm(x, g_ple) @ w_ple_gate)
    x = x + gate * (p_emb.astype(x.dtype) @ w_ple_proj)
    return x, ssm_new, conv_new.astype(x.dtype), pool_new


def setup_inputs(seed: int = 0) -> dict:
    key = jax.random.key(seed)
    ks = iter(jax.random.split(key, 40))

    def nrm(shape, scale):
        return scale * jax.random.normal(next(ks), shape, jnp.float32)

    def gain(shape):
        return 1.0 + nrm(shape, 0.05)

    a_log = jnp.log(jax.random.uniform(next(ks), (DEPTH, N_SSD_HEADS), jnp.float32, 1.0, 16.0))
    dt0 = jnp.exp(jax.random.uniform(next(ks), (DEPTH, N_SSD_HEADS), jnp.float32,
                                     math.log(1e-3), math.log(1e-1)))
    dt_bias = dt0 + jnp.log(-jnp.expm1(-dt0))
    return {
        'x_prompt': nrm((BATCH, SEQ, D_MODEL), 1.0),
        'x_sample': nrm((DEC_BATCH, DEC_SEQ, D_MODEL), 1.0),
        'p_prompt': nrm((DEPTH, BATCH, SEQ, D_PLE), 1.0),
        'p_sample': nrm((DEPTH, DEC_BATCH, DEC_SEQ, D_PLE), 1.0),
        'state_ssm': nrm((DEPTH, DEC_BATCH, N_SSD_HEADS, HEAD_DIM, D_STATE), 0.1),
        'state_conv': nrm((DEPTH, DEC_BATCH, CONV_WIDTH - 1, CONV_DIM), 1.0),
        'state_pool': nrm((DEPTH, DEC_BATCH, POOL_KEEP, D_POOL), 1.0),
        'g_mix': gain((DEPTH, D_MODEL)),
        'w_in': nrm((DEPTH, D_MODEL, IN_COLS), D_MODEL ** -0.5),
        'conv_w': nrm((DEPTH, CONV_WIDTH, CONV_DIM), CONV_WIDTH ** -0.5),
        'conv_b': nrm((DEPTH, CONV_DIM), 0.01),
        'dt_bias': dt_bias,
        'a_log': a_log,
        'd_skip': gain((DEPTH, N_SSD_HEADS)),
        'g_ssd_out': gain((DEPTH, D_INNER)),
        'w_ssd_out': nrm((DEPTH, D_INNER, D_MODEL), D_INNER ** -0.5),
        'w_pool_mix': nrm((DEPTH, N_POOL_GROUPS, POOL_GROUP, POOL_GROUP), POOL_GROUP ** -0.5),
        'pool_scale': gain((DEPTH, D_POOL)),
        'w_pool_out': nrm((DEPTH, D_POOL, D_MODEL), D_POOL ** -0.5),
        'w_o': nrm((DEPTH, D_MODEL, D_MODEL), D_MODEL ** -0.5),
        'g_ffn': gain((DEPTH, D_MODEL)),
        'w_up': nrm((DEPTH, D_MODEL, D_FF), D_MODEL ** -0.5),
        'w_down': nrm((DEPTH, D_FF, D_MODEL), D_FF ** -0.5),
        'g_ple': gain((DEPTH, D_MODEL)),
        'w_ple_gate': nrm((DEPTH, D_MODEL, D_MODEL), D_MODEL ** -0.5),
        'w_ple_proj': nrm((DEPTH, D_PLE, D_MODEL), D_PLE ** -0.5),
        'g_final': gain((D_MODEL,)),
    }


def reference(x_prompt, x_sample, p_prompt, p_sample, state_ssm, state_conv, state_pool,
              g_mix, w_in, conv_w, conv_b, dt_bias, a_log, d_skip, g_ssd_out, w_ssd_out,
              w_pool_mix, pool_scale, w_pool_out, w_o, g_ffn, w_up, w_down, g_ple, w_ple_gate, w_ple_proj,
              g_final):
    layer_w = (g_mix, w_in, conv_w, conv_b, dt_bias, a_log, d_skip, g_ssd_out, w_ssd_out,
               w_pool_mix, pool_scale, w_pool_out, w_o, g_ffn, w_up, w_down, g_ple, w_ple_gate, w_ple_proj)

    def run(x, pe, ssm0, conv0, pool0, n_past):
        ssm_out, conv_out, pool_out = [], [], []
        for i in range(DEPTH):
            x, s, c, pl = decoder_layer(x, pe[i], ssm0[i], conv0[i], pool0[i], n_past,
                                        *[w[i] for w in layer_w])
            ssm_out.append(s)
            conv_out.append(c)
            pool_out.append(pl)
        return rms_norm(x, g_final), jnp.stack(ssm_out), jnp.stack(conv_out), jnp.stack(pool_out)

    bp = x_prompt.shape[0]
    ssm0_p = jnp.zeros((DEPTH, bp, N_SSD_HEADS, HEAD_DIM, D_STATE), jnp.float32)
    conv0_p = jnp.zeros((DEPTH, bp, CONV_WIDTH - 1, CONV_DIM), x_prompt.dtype)
    pool0_p = jnp.zeros((DEPTH, bp, POOL_KEEP, D_POOL), x_prompt.dtype)
    y_prompt, ssm_p, conv_p, pool_p = run(x_prompt, p_prompt, ssm0_p, conv0_p, pool0_p, 0)
    y_sample, ssm_s, conv_s, pool_s = run(x_sample, p_sample, state_ssm, state_conv, state_pool, PAST_LEN)
    return (y_prompt, y_sample, ssm_p, conv_p, pool_p, ssm_s, conv_s, pool_s)
```

```python
import functools

import jax
import jax.numpy as jnp
from jax import lax
from jax.experimental import pallas as pl
from jax.experimental.pallas import tpu as pltpu

f32 = jnp.float32
bf16 = jnp.bfloat16

D_MODEL = 4096
D_INNER = 8192
HEAD_DIM = 64
N_HEADS = 128
N_GROUPS = 8
HEADS_PER_GROUP = N_HEADS // N_GROUPS
GROUP_W = D_INNER // N_GROUPS
D_STATE = 128
CONV_K = 4
CONV_DIM = D_INNER + 2 * N_GROUPS * D_STATE
XBC_W = GROUP_W + 2 * D_STATE
POOL_WINDOWS = (2, 4, 8, 16)
POOL_GROUP = 1024
POOL_KEEP = 15
D_FF = 16384
D_PLE = 256
EPS = 1e-6
SSD_NORM_EPS = 1e-5
CHUNK = 128

LANES = 128
SUBLANES = 8
VMEM_LIMIT = 56 * 1024 * 1024


def _params(sem):
    return pltpu.CompilerParams(dimension_semantics=sem, vmem_limit_bytes=VMEM_LIMIT)


def _rmsnorm_kernel(x_ref, g_ref, o_ref, *, eps):
    x = x_ref[...]
    ms = jnp.mean(x * x, axis=-1, keepdims=True)
    o_ref[...] = (x * lax.rsqrt(ms + eps) * g_ref[...]).astype(o_ref.dtype)


def rmsnorm(x, g, out_dtype, *, tm=512, row_block0=0, rows=None):
    m, d = x.shape
    rows = m if rows is None else rows
    return pl.pallas_call(
        functools.partial(_rmsnorm_kernel, eps=EPS),
        grid=(rows // tm,),
        in_specs=[pl.BlockSpec((tm, d), lambda i: (i + row_block0, 0)),
                  pl.BlockSpec((1, d), lambda i: (0, 0))],
        out_specs=pl.BlockSpec((tm, d), lambda i: (i, 0)),
        out_shape=jax.ShapeDtypeStruct((rows, d), out_dtype),
        compiler_params=_params(("parallel",)),
    )(x, g.reshape(1, d))


def _mm_kernel(*refs, nk, n_extra, epilogue):
    a_ref, w_ref = refs[0], refs[1]
    extras = refs[2:2 + n_extra]
    o_ref = refs[2 + n_extra]
    part = jnp.dot(a_ref[...], w_ref[...], preferred_element_type=f32)

    def finish(acc):
        o_ref[...] = epilogue(acc, *[e[...] for e in extras]).astype(o_ref.dtype)

    if nk == 1:
        finish(part)
    else:
        acc_ref = refs[-1]
        k = pl.program_id(2)

        @pl.when(k == 0)
        def _():
            acc_ref[...] = part

        @pl.when(k > 0)
        def _():
            acc_ref[...] += part

        @pl.when(k == nk - 1)
        def _():
            finish(acc_ref[...])


def matmul(a, w, *, tm, tn, tk=None, out_dtype=f32, epilogue=None, extras=()):
    m, kdim = a.shape
    n = w.shape[1]
    tk = kdim if tk is None else tk
    nk = kdim // tk
    if epilogue is None:
        epilogue = lambda acc: acc
    in_specs = [pl.BlockSpec((tm, tk), lambda i, j, k: (i, k)),
                pl.BlockSpec((tk, tn), lambda i, j, k: (k, j))]
    in_specs += [pl.BlockSpec(blk, imap) for (_, blk, imap) in extras]
    return pl.pallas_call(
        functools.partial(_mm_kernel, nk=nk, n_extra=len(extras), epilogue=epilogue),
        grid=(m // tm, n // tn, nk),
        in_specs=in_specs,
        out_specs=pl.BlockSpec((tm, tn), lambda i, j, k: (i, j)),
        out_shape=jax.ShapeDtypeStruct((m, n), out_dtype),
        scratch_shapes=[pltpu.VMEM((tm, tn), f32)] if nk > 1 else [],
        compiler_params=_params(("parallel", "parallel", "arbitrary")),
    )(a, w, *[e[0] for e in extras])


def _tile(arr, tm, tn):
    return (arr, (tm, tn), lambda i, j, k: (i, j))


def _silu(x):
    return x * jax.nn.sigmoid(x)


def _softplus(x):
    return jnp.maximum(x, 0.0) + jnp.log1p(jnp.exp(-jnp.abs(x)))


def _expand_heads(x, rows):
    lane = lax.broadcasted_iota(jnp.int32, (rows, LANES), 1)
    tiles = []
    for k in range(HEADS_PER_GROUP // 2):
        a = jnp.broadcast_to(x[:, 2 * k:2 * k + 1], (rows, LANES))
        b = jnp.broadcast_to(x[:, 2 * k + 1:2 * k + 2], (rows, LANES))
        tiles.append(jnp.where(lane < HEAD_DIM, a, b))
    return jnp.concatenate(tiles, axis=1)


def _conv_silu(xpad, cw_refs, cb_refs, q):
    w = jnp.concatenate([r[...] for r in cw_refs], axis=1)
    bias = jnp.concatenate([r[...] for r in cb_refs], axis=1)
    conv = bias
    for k in range(CONV_K):
        conv = conv + w[k:k + 1, :] * xpad[pl.ds(SUBLANES - (CONV_K - 1) + k, q), :]
    return _silu(conv)


def _stage_rows(xpad, xs_ref, b_ref, c_ref, q):
    xpad[pl.ds(SUBLANES, q), 0:GROUP_W] = xs_ref[...]
    xpad[pl.ds(SUBLANES, q), GROUP_W:GROUP_W + D_STATE] = b_ref[...]
    xpad[pl.ds(SUBLANES, q), GROUP_W + D_STATE:XBC_W] = c_ref[...]


def _stage_prev(xpad, pxs_ref, pb_ref, pc_ref):
    r0 = SUBLANES - (CONV_K - 1)
    xpad[pl.ds(r0, CONV_K - 1), 0:GROUP_W] = pxs_ref[0]
    xpad[pl.ds(r0, CONV_K - 1), GROUP_W:GROUP_W + D_STATE] = pb_ref[0]
    xpad[pl.ds(r0, CONV_K - 1), GROUP_W + D_STATE:XBC_W] = pc_ref[0]


def _gate_norm_store(y, xs, dsk_ref, zs_ref, gout_ref, yn_ref):
    y = y + dsk_ref[...] * xs
    y = y * zs_ref[...].astype(f32)
    ms = jnp.mean(y * y, axis=-1, keepdims=True)
    yn_ref[...] = (y * lax.rsqrt(ms + SSD_NORM_EPS) * gout_ref[...]).astype(yn_ref.dtype)


def _decay_rows(cd_b):
    return jnp.concatenate(
        [jnp.broadcast_to(cd_b[h:h + 1, :], (HEAD_DIM, D_STATE)) for h in range(HEADS_PER_GROUP)], axis=0)


def _ssd_prompt_kernel(xs_ref, b_ref, c_ref, pxs_ref, pb_ref, pc_ref,
                       cwx_ref, cwb_ref, cwc_ref, cbx_ref, cbb_ref, cbc_ref,
                       dt_ref, dtt_ref, dtb_ref, dtbt_ref, alog_ref, alogt_ref,
                       dsk_ref, zs_ref, gout_ref,
                       yn_ref, ssm_ref, xpad, state, *, nc):
    q = CHUNK
    c = pl.program_id(2)
    hi = lax.Precision.HIGHEST

    @pl.when(c == 0)
    def _():
        state[...] = jnp.zeros_like(state)
        _stage_prev(xpad, pxs_ref, pb_ref, pc_ref)

    @pl.when(c > 0)
    def _():
        xpad[pl.ds(SUBLANES - 3, 3), :] = xpad[pl.ds(SUBLANES + q - 3, 3), :]

    _stage_rows(xpad, xs_ref, b_ref, c_ref, q)
    act = _conv_silu(xpad, (cwx_ref, cwb_ref, cwc_ref), (cbx_ref, cbb_ref, cbc_ref), q)
    xs = act[:, 0:GROUP_W]
    bm = act[:, GROUP_W:GROUP_W + D_STATE].astype(bf16)
    cm = act[:, GROUP_W + D_STATE:XBC_W].astype(bf16)

    dt = _softplus(dt_ref[0] + dtb_ref[0])
    dta = dt * (-jnp.exp(alog_ref[0]))
    dtt = _softplus(dtt_ref[0] + dtbt_ref[0])
    dtat = dtt * (-jnp.exp(alogt_ref[0]))
    row = lax.broadcasted_iota(jnp.int32, (q, q), 0)
    col = lax.broadcasted_iota(jnp.int32, (q, q), 1)
    causal = row >= col
    acum = jnp.dot(causal.astype(f32), dta, precision=hi, preferred_element_type=f32)
    acum_t = jnp.dot(dtat, (row <= col).astype(f32), precision=hi, preferred_element_type=f32)
    tot_b = jnp.dot(dtat, jnp.ones((q, D_STATE), f32), precision=hi, preferred_element_type=f32)

    dt_x = _expand_heads(dt, q)
    acum_x = _expand_heads(acum, q)
    xdt = xs * dt_x
    decay_x = jnp.exp(acum_x[q - 1:q, :] - acum_x)
    xdtd = (xdt * decay_x).astype(bf16)
    xdt_b = xdt.astype(bf16)

    cb = lax.dot_general(cm, bm, (((1,), (1,)), ((), ())), preferred_element_type=f32)
    lane = lax.broadcasted_iota(jnp.int32, (q, LANES), 1)
    tiles = []
    for k in range(HEADS_PER_GROUP // 2):
        res = []
        for h in (2 * k, 2 * k + 1):
            seg = jnp.broadcast_to(acum[:, h:h + 1], (q, q)) - jnp.broadcast_to(acum_t[h:h + 1, :], (q, q))
            m_h = (cb * jnp.where(causal, jnp.exp(seg), 0.0)).astype(bf16)
            res.append(jnp.dot(m_h, xdt_b[:, k * LANES:(k + 1) * LANES], preferred_element_type=f32))
        tiles.append(jnp.where(lane < HEAD_DIM, res[0], res[1]))
    y_diag = jnp.concatenate(tiles, axis=1)

    s_prev = state[...]
    y_off = lax.dot_general(cm, s_prev.astype(bf16), (((1,), (1,)), ((), ())), preferred_element_type=f32)
    y_off = y_off * jnp.exp(acum_x)
    new_states = lax.dot_general(xdtd, bm, (((0,), (0,)), ((), ())), preferred_element_type=f32)
    s_new = s_prev * _decay_rows(jnp.exp(tot_b)) + new_states
    state[...] = s_new

    @pl.when(c == nc - 1)
    def _():
        ssm_ref[0, 0] = s_new

    _gate_norm_store(y_diag + y_off, xs, dsk_ref, zs_ref, gout_ref, yn_ref)


def _ssd_sample_kernel(xs_ref, b_ref, c_ref, pxs_ref, pb_ref, pc_ref,
                       cwx_ref, cwb_ref, cwc_ref, cbx_ref, cbb_ref, cbc_ref,
                       dt_ref, dtb_ref, alog_ref, dsk_ref, zs_ref, gout_ref, s0_ref,
                       yn_ref, ssm_ref, xpad, *, q):
    hi = lax.Precision.HIGHEST
    _stage_prev(xpad, pxs_ref, pb_ref, pc_ref)
    _stage_rows(xpad, xs_ref, b_ref, c_ref, q)
    act = _conv_silu(xpad, (cwx_ref, cwb_ref, cwc_ref), (cbx_ref, cbb_ref, cbc_ref), q)
    xs = act[:, 0:GROUP_W]
    bm = act[:, GROUP_W:GROUP_W + D_STATE].astype(bf16)
    cm = act[:, GROUP_W + D_STATE:XBC_W].astype(bf16)

    dt = _softplus(dt_ref[0] + dtb_ref[0])
    dta = dt * (-jnp.exp(alog_ref[0]))
    row = lax.broadcasted_iota(jnp.int32, (q, q), 0)
    col = lax.broadcasted_iota(jnp.int32, (q, q), 1)
    acum = jnp.dot((row >= col).astype(f32), dta, precision=hi, preferred_element_type=f32)
    tot_b = lax.dot_general(dta, jnp.ones((q, D_STATE), f32), (((0,), (0,)), ((), ())),
                            precision=hi, preferred_element_type=f32)

    dt_x = _expand_heads(dt, q)
    acum_x = _expand_heads(acum, q)
    xdt = xs * dt_x
    decay_x = jnp.exp(acum_x[q - 1:q, :] - acum_x)
    xdtd = (xdt * decay_x).astype(bf16)

    cb = lax.dot_general(cm, bm, (((1,), (1,)), ((), ())), preferred_element_type=f32)
    rowx = lax.broadcasted_iota(jnp.int32, (q, GROUP_W), 0)
    y_diag = jnp.zeros((q, GROUP_W), f32)
    for j in range(q):
        wj = jnp.where(rowx >= j, jnp.exp(acum_x - acum_x[j:j + 1, :]), 0.0)
        wj = wj * jnp.broadcast_to(cb[:, j:j + 1], (q, GROUP_W))
        y_diag = y_diag + wj * xdt[j:j + 1, :]

    s_prev = s0_ref[0, 0]
    y_off = lax.dot_general(cm, s_prev.astype(bf16), (((1,), (1,)), ((), ())), preferred_element_type=f32)
    y_off = y_off * jnp.exp(acum_x)
    new_states = lax.dot_general(xdtd, bm, (((0,), (0,)), ((), ())), preferred_element_type=f32)
    ssm_ref[0, 0] = s_prev * _decay_rows(jnp.exp(tot_b)) + new_states

    _gate_norm_store(y_diag + y_off, xs, dsk_ref, zs_ref, gout_ref, yn_ref)


def _group_param(v):
    return v.reshape(N_GROUPS, 1, HEADS_PER_GROUP), v.reshape(N_GROUPS, HEADS_PER_GROUP, 1)


def ssd_branch(xbc, dt_raw, zs, conv_prev, s0, conv_w, conv_b, dt_bias, a_log, d_skip, g_out,
               *, row0, nb, seq):
    m = xbc.shape[0]
    q = min(seq, CHUNK)
    nc = seq // q
    rb0 = row0 // q
    dtg = dt_raw.reshape(m, N_GROUPS, HEADS_PER_GROUP).transpose(1, 0, 2)
    dtb, dtbt = _group_param(dt_bias)
    alog, alogt = _group_param(a_log)
    dsk = jnp.repeat(d_skip, HEAD_DIM).reshape(1, D_INNER)
    gout = g_out.reshape(1, D_INNER)
    cb2 = conv_b.reshape(1, CONV_DIM)
    nbg, ncg = D_INNER // D_STATE, (D_INNER + N_GROUPS * D_STATE) // D_STATE

    if s0 is None:
        rowblk = lambda b, g, c: rb0 + b * nc + c
        grid = (nb, N_GROUPS, nc)
        ix = lambda f: (lambda b, g, c: f(b, g, rowblk(b, g, c)))
        sem = ("parallel", "parallel", "arbitrary")
    else:
        assert nc == 1
        grid = (nb, N_GROUPS)
        ix = lambda f: (lambda b, g: f(b, g, rb0 + b))
        sem = ("parallel", "parallel")

    in_specs = [
        pl.BlockSpec((q, GROUP_W), ix(lambda b, g, r: (r, g))),
        pl.BlockSpec((q, D_STATE), ix(lambda b, g, r: (r, nbg + g))),
        pl.BlockSpec((q, D_STATE), ix(lambda b, g, r: (r, ncg + g))),
        pl.BlockSpec((1, CONV_K - 1, GROUP_W), ix(lambda b, g, r: (b, 0, g))),
        pl.BlockSpec((1, CONV_K - 1, D_STATE), ix(lambda b, g, r: (b, 0, nbg + g))),
        pl.BlockSpec((1, CONV_K - 1, D_STATE), ix(lambda b, g, r: (b, 0, ncg + g))),
        pl.BlockSpec((CONV_K, GROUP_W), ix(lambda b, g, r: (0, g))),
        pl.BlockSpec((CONV_K, D_STATE), ix(lambda b, g, r: (0, nbg + g))),
        pl.BlockSpec((CONV_K, D_STATE), ix(lambda b, g, r: (0, ncg + g))),
        pl.BlockSpec((1, GROUP_W), ix(lambda b, g, r: (0, g))),
        pl.BlockSpec((1, D_STATE), ix(lambda b, g, r: (0, nbg + g))),
        pl.BlockSpec((1, D_STATE), ix(lambda b, g, r: (0, ncg + g))),
        pl.BlockSpec((1, q, HEADS_PER_GROUP), ix(lambda b, g, r: (g, r, 0))),
    ]
    args = [xbc, xbc, xbc, conv_prev, conv_prev, conv_prev, conv_w, conv_w, conv_w, cb2, cb2, cb2, dtg]
    vec16 = pl.BlockSpec((1, 1, HEADS_PER_GROUP), ix(lambda b, g, r: (g, 0, 0)))
    vec16t = pl.BlockSpec((1, HEADS_PER_GROUP, 1), ix(lambda b, g, r: (g, 0, 0)))
    tail_specs = [
        pl.BlockSpec((1, GROUP_W), ix(lambda b, g, r: (0, g))),
        pl.BlockSpec((q, GROUP_W), ix(lambda b, g, r: (r, g))),
        pl.BlockSpec((1, GROUP_W), ix(lambda b, g, r: (0, g))),
    ]
    state_spec = pl.BlockSpec((1, 1, GROUP_W, D_STATE), ix(lambda b, g, r: (b, g, 0, 0)))
    yn_spec = pl.BlockSpec((q, GROUP_W), ix(lambda b, g, r: (r - rb0, g)))
    out_shape = (jax.ShapeDtypeStruct((nb * seq, D_INNER), bf16),
                 jax.ShapeDtypeStruct((nb, N_GROUPS, GROUP_W, D_STATE), f32))
    xpad = pltpu.VMEM((SUBLANES + q, XBC_W), f32)

    if s0 is None:
        dtgt = dt_raw.reshape(m, N_GROUPS, HEADS_PER_GROUP).transpose(1, 2, 0)
        in_specs += [pl.BlockSpec((1, HEADS_PER_GROUP, q), ix(lambda b, g, r: (g, 0, r))),
                     vec16, vec16t, vec16, vec16t] + tail_specs
        args += [dtgt, dtb, dtbt, alog, alogt, dsk, zs, gout]
        kern = functools.partial(_ssd_prompt_kernel, nc=nc)
        scratch = [xpad, pltpu.VMEM((GROUP_W, D_STATE), f32)]
    else:
        in_specs += [vec16, vec16] + tail_specs + [state_spec]
        args += [dtb, alog, dsk, zs, gout, s0.reshape(nb, N_GROUPS, GROUP_W, D_STATE)]
        kern = functools.partial(_ssd_sample_kernel, q=q)
        scratch = [xpad]

    yn, ssm = pl.pallas_call(
        kern, grid=grid, in_specs=in_specs, out_specs=(yn_spec, state_spec), out_shape=out_shape,
        scratch_shapes=scratch, compiler_params=_params(sem),
    )(*args)
    return yn, ssm.reshape(nb, N_HEADS, HEAD_DIM, D_STATE)


def _pool_kernel(u_ref, prev_ref, o_ref, halo, *, tl, n_past):
    lt = pl.program_id(1)
    base = 2 * SUBLANES

    @pl.when(lt == 0)
    def _():
        halo[pl.ds(base - POOL_KEEP, POOL_KEEP), :] = prev_ref[0]

    @pl.when(lt > 0)
    def _():
        halo[pl.ds(base - POOL_KEEP, POOL_KEEP), :] = halo[pl.ds(base + tl - POOL_KEEP, POOL_KEEP), :]

    halo[pl.ds(base, tl), :] = u_ref[...]
    t = lt * tl + lax.broadcasted_iota(jnp.int32, (tl, 1), 0) + (n_past + 1)
    for gi, w in enumerate(POOL_WINDOWS):
        cols = slice(gi * POOL_GROUP, (gi + 1) * POOL_GROUP)
        cur = halo[pl.ds(base, tl), cols]
        acc = cur
        for k in range(1, w):
            acc = acc + halo[pl.ds(base - k, tl), cols]
        cnt = jnp.minimum(t, w).astype(f32)
        o_ref[:, cols] = (acc / cnt - cur).astype(o_ref.dtype)


def pool_branch(u, prev, *, row0, nb, seq, n_past, tl):
    d = u.shape[1]
    nlt = seq // tl
    rb0 = row0 // tl
    return pl.pallas_call(
        functools.partial(_pool_kernel, tl=tl, n_past=n_past),
        grid=(nb, nlt),
        in_specs=[pl.BlockSpec((tl, d), lambda b, l: (rb0 + b * nlt + l, 0)),
                  pl.BlockSpec((1, POOL_KEEP, d), lambda b, l: (b, 0, 0))],
        out_specs=pl.BlockSpec((tl, d), lambda b, l: (b * nlt + l, 0)),
        out_shape=jax.ShapeDtypeStruct((nb * seq, d), bf16),
        scratch_shapes=[pltpu.VMEM((2 * SUBLANES + tl, d), f32)],
        compiler_params=_params(("parallel", "arbitrary")),
    )(u, prev)


def _poolmix_kernel(a_ref, w_ref, s_ref, o_ref):
    acc = jnp.dot(a_ref[...], w_ref[0], preferred_element_type=f32)
    o_ref[...] = (acc * s_ref[...]).astype(o_ref.dtype)


def pool_mix(pooled, w_mix, scale, *, tm):
    m, d = pooled.shape
    ng = d // POOL_GROUP
    return pl.pallas_call(
        _poolmix_kernel,
        grid=(m // tm, ng),
        in_specs=[pl.BlockSpec((tm, POOL_GROUP), lambda i, g: (i, g)),
                  pl.BlockSpec((1, POOL_GROUP, POOL_GROUP), lambda i, g: (g, 0, 0)),
                  pl.BlockSpec((1, POOL_GROUP), lambda i, g: (0, g))],
        out_specs=pl.BlockSpec((tm, POOL_GROUP), lambda i, g: (i, g)),
        out_shape=jax.ShapeDtypeStruct((m, d), bf16),
        compiler_params=_params(("parallel", "parallel")),
    )(pooled, w_mix, scale.reshape(1, d))


def _ple_epilogue(acc, x2, p, wp):
    pe = jnp.dot(p, wp, preferred_element_type=f32)
    return x2 + jax.nn.sigmoid(acc) * pe


def kernel(x_prompt, x_sample, p_prompt, p_sample, state_ssm, state_conv, state_pool, g_mix, w_in, conv_w, conv_b, dt_bias, a_log, d_skip, g_ssd_out, w_ssd_out, w_pool_mix, pool_scale, w_pool_out, w_o, g_ffn, w_up, w_down, g_ple, w_ple_gate, w_ple_proj, g_final):
    bp, lp, d = x_prompt.shape
    bs, ls, _ = x_sample.shape
    mp, ms = bp * lp, bs * ls
    n_past = 16384

    x = jnp.concatenate([x_prompt.reshape(mp, d), x_sample.reshape(ms, d)], axis=0)
    pe = jnp.concatenate([p_prompt[0].reshape(mp, D_PLE), p_sample[0].reshape(ms, D_PLE)], axis=0).astype(bf16)

    c0, c1, c2, c3 = D_INNER, D_INNER + CONV_DIM, D_INNER + CONV_DIM + N_HEADS, D_INNER + CONV_DIM + N_HEADS + D_MODEL
    w_in0 = w_in[0]
    w_z = w_in0[:, :c0].astype(bf16)
    w_xbc = w_in0[:, c0:c1].astype(bf16)
    w_dt = w_in0[:, c1:c2].astype(bf16)
    w_u = w_in0[:, c2:c3].astype(bf16)
    w_gate = w_in0[:, c3:].astype(bf16)

    h = rmsnorm(x, g_mix[0], bf16)
    tm = 1024
    zs = matmul(h, w_z, tm=tm, tn=512, out_dtype=bf16, epilogue=_silu)
    xbc = matmul(h, w_xbc, tm=tm, tn=512)
    dt_raw = matmul(h, w_dt, tm=tm, tn=N_HEADS)
    u = matmul(h, w_u, tm=tm, tn=512)
    sg = matmul(h, w_gate, tm=tm, tn=512, out_dtype=bf16, epilogue=jax.nn.sigmoid)

    ssd_w = (conv_w[0], conv_b[0], dt_bias[0], a_log[0], d_skip[0], g_ssd_out[0])
    conv0_p = jnp.zeros((bp, CONV_K - 1, CONV_DIM), f32)
    yn_p, ssm_p = ssd_branch(xbc, dt_raw, zs, conv0_p, None, *ssd_w, row0=0, nb=bp, seq=lp)
    yn_s, ssm_s = ssd_branch(xbc, dt_raw, zs, state_conv[0], state_ssm[0], *ssd_w, row0=mp, nb=bs, seq=ls)
    yn = jnp.concatenate([yn_p, yn_s], axis=0)

    pool0_p = jnp.zeros((bp, POOL_KEEP, d), f32)
    pooled = jnp.concatenate([
        pool_branch(u, pool0_p, row0=0, nb=bp, seq=lp, n_past=0, tl=256),
        pool_branch(u, state_pool[0], row0=mp, nb=bs, seq=ls, n_past=n_past, tl=ls)], axis=0)
    pm = pool_mix(pooled, w_pool_mix[0].astype(bf16), pool_scale[0], tm=tm)

    ta = matmul(yn, w_ssd_out[0].astype(bf16), tm=512, tn=512,
                epilogue=lambda acc, ga: ga.astype(f32) * acc,
                extras=[(sg, (512, 512), lambda i, j, k: (i, j))])
    merged = matmul(pm, w_pool_out[0].astype(bf16), tm=tm, tn=512, out_dtype=bf16,
                    epilogue=lambda acc, gb, t: t + gb.astype(f32) * acc,
                    extras=[(sg, (tm, 512), lambda i, j, k: (i, j + D_MODEL // 512)), _tile(ta, tm, 512)])
    x1 = matmul(merged, w_o[0].astype(bf16), tm=tm, tn=512,
                epilogue=lambda acc, r: r + acc, extras=[_tile(x, tm, 512)])

    h2 = rmsnorm(x1, g_ffn[0], bf16)
    act = matmul(h2, w_up[0].astype(bf16), tm=tm, tn=512, out_dtype=bf16,
                 epilogue=lambda acc: jnp.square(jnp.maximum(acc, 0.0)))
    x2 = matmul(act, w_down[0].astype(bf16), tm=tm, tn=512, tk=2048,
                epilogue=lambda acc, r: r + acc, extras=[_tile(x1, tm, 512)])

    h3 = rmsnorm(x2, g_ple[0], bf16)
    x3 = matmul(h3, w_ple_gate[0].astype(bf16), tm=tm, tn=512, epilogue=_ple_epilogue,
                extras=[_tile(x2, tm, 512),
                        (pe, (tm, D_PLE), lambda i, j, k: (i, 0)),
                        (w_ple_proj[0].astype(bf16), (D_PLE, 512), lambda i, j, k: (0, j))])

    y_p = rmsnorm(x3, g_final, f32, rows=mp).reshape(bp, lp, d)
    y_s = rmsnorm(x3, g_final, f32, row_block0=mp // 512, rows=ms).reshape(bs, ls, d)

    xbc_p = xbc[:mp].reshape(bp, lp, CONV_DIM)
    u_p = u[:mp].reshape(bp, lp, d)
    conv_p = xbc_p[:, lp - (CONV_K - 1):]
    pool_p = u_p[:, lp - POOL_KEEP:]
    xbc_s = xbc[mp:].reshape(bs, ls, CONV_DIM)
    u_s = u[mp:].reshape(bs, ls, d)
    conv_s = jnp.concatenate([state_conv[0], xbc_s], axis=1)[:, -(CONV_K - 1):]
    pool_s = jnp.concatenate([state_pool[0], u_s], axis=1)[:, -POOL_KEEP:]
    return (y_p, y_s, ssm_p[None], conv_p[None], pool_p[None], ssm_s[None], conv_s[None], pool_s[None])
```

```python
import functools

import jax
import jax.numpy as jnp
from jax import lax
from jax.experimental import pallas as pl
from jax.experimental.pallas import tpu as pltpu

f32 = jnp.float32
bf16 = jnp.bfloat16

D_MODEL = 4096
D_INNER = 8192
HEAD_DIM = 64
N_HEADS = 128
N_GROUPS = 8
HEADS_PER_GROUP = N_HEADS // N_GROUPS
GROUP_W = D_INNER // N_GROUPS
D_STATE = 128
CONV_K = 4
CONV_DIM = D_INNER + 2 * N_GROUPS * D_STATE
XBC_W = GROUP_W + 2 * D_STATE
POOL_WINDOWS = (2, 4, 8, 16)
POOL_GROUP = 1024
POOL_KEEP = 15
D_PLE = 256
EPS = 1e-6
SSD_NORM_EPS = 1e-5
CHUNK = 128
PAST_LEN = 16384

LANES = 128
SUBLANES = 8
VMEM_LIMIT = 56 * 1024 * 1024
TM = 1024
TN = 512
SAMPLE_SEQS_PER_STEP = 4


def _params(sem):
    return pltpu.CompilerParams(dimension_semantics=sem, vmem_limit_bytes=VMEM_LIMIT)


def _rms(x, g, eps):
    ms = jnp.mean(x * x, axis=-1, keepdims=True)
    return x * lax.rsqrt(ms + eps) * g


def _rmsnorm_kernel(x_ref, g_ref, o_ref, *, eps):
    o_ref[...] = _rms(x_ref[...], g_ref[...], eps).astype(o_ref.dtype)


def rmsnorm(x, g, out_dtype, *, name, tm=512, row_block0=0, rows=None):
    m, d = x.shape
    rows = m if rows is None else rows
    return pl.pallas_call(
        functools.partial(_rmsnorm_kernel, eps=EPS),
        grid=(rows // tm,),
        in_specs=[pl.BlockSpec((tm, d), lambda i: (i + row_block0, 0)),
                  pl.BlockSpec((1, d), lambda i: (0, 0))],
        out_specs=pl.BlockSpec((tm, d), lambda i: (i, 0)),
        out_shape=jax.ShapeDtypeStruct((rows, d), out_dtype),
        compiler_params=_params(("parallel",)),
        name=name,
    )(x, g.reshape(1, d))


def _rmsnorm2_kernel(xp_ref, xs_ref, g_ref, o_ref, *, eps, n_p):
    i = pl.program_id(0)

    @pl.when(i < n_p)
    def _():
        o_ref[...] = _rms(xp_ref[...], g_ref[...], eps).astype(o_ref.dtype)

    @pl.when(i >= n_p)
    def _():
        o_ref[...] = _rms(xs_ref[...], g_ref[...], eps).astype(o_ref.dtype)


def rmsnorm_stacked(x_p, x_s, g, out_dtype, *, name, tm=512):
    (mp, d), ms = x_p.shape, x_s.shape[0]
    n_p = mp // tm
    return pl.pallas_call(
        functools.partial(_rmsnorm2_kernel, eps=EPS, n_p=n_p),
        grid=((mp + ms) // tm,),
        in_specs=[pl.BlockSpec((tm, d), lambda i: (jnp.minimum(i, n_p - 1), 0)),
                  pl.BlockSpec((tm, d), lambda i: (jnp.maximum(i - n_p, 0), 0)),
                  pl.BlockSpec((1, d), lambda i: (0, 0))],
        out_specs=pl.BlockSpec((tm, d), lambda i: (i, 0)),
        out_shape=jax.ShapeDtypeStruct((mp + ms, d), out_dtype),
        compiler_params=_params(("arbitrary",)),
        name=name,
    )(x_p, x_s, g.reshape(1, d))


def _mm_kernel(*refs, nk, n_extra, epilogue):
    a_ref, w_ref = refs[0], refs[1]
    extras = refs[2:2 + n_extra]
    o_ref = refs[2 + n_extra]
    part = jnp.dot(a_ref[...], w_ref[...].astype(bf16), preferred_element_type=f32)

    def finish(acc):
        o_ref[...] = epilogue(acc, *[e[...] for e in extras]).astype(o_ref.dtype)

    if nk == 1:
        finish(part)
    else:
        acc_ref = refs[-1]
        k = pl.program_id(2)

        @pl.when(k == 0)
        def _():
            acc_ref[...] = part

        @pl.when(k > 0)
        def _():
            acc_ref[...] += part

        @pl.when(k == nk - 1)
        def _():
            finish(acc_ref[...])


def matmul(a, w, *, name, n=None, col0=0, tm=TM, tn=TN, tk=None, out_dtype=f32, epilogue=None, extras=()):
    m, kdim = a.shape
    n = w.shape[1] if n is None else n
    tk = kdim if tk is None else tk
    nk = kdim // tk
    if epilogue is None:
        epilogue = lambda acc: acc
    if col0 % tn == 0:
        w_spec = pl.BlockSpec((tk, tn), lambda i, j, k: (k, col0 // tn + j))
    else:
        w_spec = pl.BlockSpec((pl.Element(tk), pl.Element(tn)),
                              lambda i, j, k: (k * tk, pl.multiple_of(col0 + j * tn, LANES)))
    in_specs = [pl.BlockSpec((tm, tk), lambda i, j, k: (i, k)), w_spec]
    in_specs += [pl.BlockSpec(blk, imap) for (_, blk, imap) in extras]
    return pl.pallas_call(
        functools.partial(_mm_kernel, nk=nk, n_extra=len(extras), epilogue=epilogue),
        grid=(m // tm, n // tn, nk),
        in_specs=in_specs,
        out_specs=pl.BlockSpec((tm, tn), lambda i, j, k: (i, j)),
        out_shape=jax.ShapeDtypeStruct((m, n), out_dtype),
        scratch_shapes=[pltpu.VMEM((tm, tn), f32)] if nk > 1 else [],
        compiler_params=_params(("parallel", "parallel", "arbitrary")),
        name=name,
    )(a, w, *[e[0] for e in extras])


def _tile(arr, tm=TM, tn=TN):
    return (arr, (tm, tn), lambda i, j, k: (i, j))


def _stacked_tiles(x_p, x_s, tm=TM, tn=TN):
    n_p = x_p.shape[0] // tm
    return [(x_p, (tm, tn), lambda i, j, k: (jnp.minimum(i, n_p - 1), j)),
            (x_s, (tm, tn), lambda i, j, k: (jnp.maximum(i - n_p, 0), j))]


def _pick_stacked(t_p, t_s, n_p):
    return jnp.where(pl.program_id(0) < n_p, t_p, t_s)


def _silu(x):
    return x * jax.nn.sigmoid(x)


def _softplus(x):
    return jnp.maximum(x, 0.0) + jnp.log1p(jnp.exp(-jnp.abs(x)))


def _expand_heads(x, rows):
    lane = lax.broadcasted_iota(jnp.int32, (rows, LANES), 1)
    tiles = []
    for k in range(HEADS_PER_GROUP // 2):
        a = jnp.broadcast_to(x[:, 2 * k:2 * k + 1], (rows, LANES))
        b = jnp.broadcast_to(x[:, 2 * k + 1:2 * k + 2], (rows, LANES))
        tiles.append(jnp.where(lane < HEAD_DIM, a, b))
    return jnp.concatenate(tiles, axis=1)


def _conv_silu(xpad, r0, cw_refs, cb_refs, q):
    w = jnp.concatenate([r[...] for r in cw_refs], axis=1)
    bias = jnp.concatenate([r[...] for r in cb_refs], axis=1)
    conv = bias
    for k in range(CONV_K):
        conv = conv + w[k:k + 1, :] * xpad[pl.ds(r0 - (CONV_K - 1) + k, q), :]
    return _silu(conv)


def _stage(xpad, r0, nrows, xs, b, c):
    xpad[pl.ds(r0, nrows), 0:GROUP_W] = xs
    xpad[pl.ds(r0, nrows), GROUP_W:GROUP_W + D_STATE] = b
    xpad[pl.ds(r0, nrows), GROUP_W + D_STATE:XBC_W] = c


def _gate_norm_store(y, xs, dsk_ref, zs_ref, gout_ref, yn_ref):
    y = y + dsk_ref[...] * xs
    y = y * zs_ref[...].astype(f32)
    ms = jnp.mean(y * y, axis=-1, keepdims=True)
    yn_ref[...] = (y * lax.rsqrt(ms + SSD_NORM_EPS) * gout_ref[...]).astype(yn_ref.dtype)


def _decay_rows(cd_b):
    return jnp.concatenate(
        [jnp.broadcast_to(cd_b[h:h + 1, :], (HEAD_DIM, D_STATE)) for h in range(HEADS_PER_GROUP)], axis=0)


def _ssd_prompt_kernel(xs_ref, b_ref, c_ref, pxs_ref, pb_ref, pc_ref,
                       cwx_ref, cwb_ref, cwc_ref, cbx_ref, cbb_ref, cbc_ref,
                       dt_ref, dtt_ref, dtb_ref, dtbt_ref, alog_ref, alogt_ref,
                       dsk_ref, zs_ref, gout_ref,
                       yn_ref, ssm_ref, xpad, state, *, nc):
    q = CHUNK
    r0 = SUBLANES
    c = pl.program_id(2)
    hi = lax.Precision.HIGHEST

    @pl.when(c == 0)
    def _():
        state[...] = jnp.zeros_like(state)
        _stage(xpad, r0 - (CONV_K - 1), CONV_K - 1, pxs_ref[0], pb_ref[0], pc_ref[0])

    @pl.when(c > 0)
    def _():
        xpad[pl.ds(r0 - 3, 3), :] = xpad[pl.ds(r0 + q - 3, 3), :]

    _stage(xpad, r0, q, xs_ref[...], b_ref[...], c_ref[...])
    act = _conv_silu(xpad, r0, (cwx_ref, cwb_ref, cwc_ref), (cbx_ref, cbb_ref, cbc_ref), q)
    xs = act[:, 0:GROUP_W]
    bm = act[:, GROUP_W:GROUP_W + D_STATE].astype(bf16)
    cm = act[:, GROUP_W + D_STATE:XBC_W].astype(bf16)

    dt = _softplus(dt_ref[0] + dtb_ref[0])
    dta = dt * (-jnp.exp(alog_ref[0]))
    dtt = _softplus(dtt_ref[0] + dtbt_ref[0])
    dtat = dtt * (-jnp.exp(alogt_ref[0]))
    row = lax.broadcasted_iota(jnp.int32, (q, q), 0)
    col = lax.broadcasted_iota(jnp.int32, (q, q), 1)
    causal = row >= col
    acum = jnp.dot(causal.astype(f32), dta, precision=hi, preferred_element_type=f32)
    acum_t = jnp.dot(dtat, (row <= col).astype(f32), precision=hi, preferred_element_type=f32)
    tot_b = jnp.dot(dtat, jnp.ones((q, D_STATE), f32), precision=hi, preferred_element_type=f32)

    dt_x = _expand_heads(dt, q)
    acum_x = _expand_heads(acum, q)
    xdt = xs * dt_x
    decay_x = jnp.exp(acum_x[q - 1:q, :] - acum_x)
    xdtd = (xdt * decay_x).astype(bf16)
    xdt_b = xdt.astype(bf16)

    cb = lax.dot_general(cm, bm, (((1,), (1,)), ((), ())), preferred_element_type=f32)
    lane = lax.broadcasted_iota(jnp.int32, (q, LANES), 1)
    tiles = []
    for k in range(HEADS_PER_GROUP // 2):
        res = []
        for h in (2 * k, 2 * k + 1):
            seg = jnp.broadcast_to(acum[:, h:h + 1], (q, q)) - jnp.broadcast_to(acum_t[h:h + 1, :], (q, q))
            m_h = (cb * jnp.where(causal, jnp.exp(seg), 0.0)).astype(bf16)
            res.append(jnp.dot(m_h, xdt_b[:, k * LANES:(k + 1) * LANES], preferred_element_type=f32))
        tiles.append(jnp.where(lane < HEAD_DIM, res[0], res[1]))
    y_diag = jnp.concatenate(tiles, axis=1)

    s_prev = state[...]
    y_off = lax.dot_general(cm, s_prev.astype(bf16), (((1,), (1,)), ((), ())), preferred_element_type=f32)
    y_off = y_off * jnp.exp(acum_x)
    new_states = lax.dot_general(xdtd, bm, (((0,), (0,)), ((), ())), preferred_element_type=f32)
    s_new = s_prev * _decay_rows(jnp.exp(tot_b)) + new_states
    state[...] = s_new

    @pl.when(c == nc - 1)
    def _():
        ssm_ref[0, 0] = s_new

    _gate_norm_store(y_diag + y_off, xs, dsk_ref, zs_ref, gout_ref, yn_ref)


def _ssd_short_chunk(xpad, r0, cw_refs, cb_refs, dt_raw, dtb, alog, s_prev, q):
    hi = lax.Precision.HIGHEST
    act = _conv_silu(xpad, r0, cw_refs, cb_refs, q)
    xs = act[:, 0:GROUP_W]
    bm = act[:, GROUP_W:GROUP_W + D_STATE].astype(bf16)
    cm = act[:, GROUP_W + D_STATE:XBC_W].astype(bf16)

    dt = _softplus(dt_raw + dtb)
    dta = dt * (-jnp.exp(alog))
    row = lax.broadcasted_iota(jnp.int32, (q, q), 0)
    col = lax.broadcasted_iota(jnp.int32, (q, q), 1)
    acum = jnp.dot((row >= col).astype(f32), dta, precision=hi, preferred_element_type=f32)
    tot_b = lax.dot_general(dta, jnp.ones((q, D_STATE), f32), (((0,), (0,)), ((), ())),
                            precision=hi, preferred_element_type=f32)

    dt_x = _expand_heads(dt, q)
    acum_x = _expand_heads(acum, q)
    xdt = xs * dt_x
    decay_x = jnp.exp(acum_x[q - 1:q, :] - acum_x)
    xdtd = (xdt * decay_x).astype(bf16)

    cb = lax.dot_general(cm, bm, (((1,), (1,)), ((), ())), preferred_element_type=f32)
    rowx = lax.broadcasted_iota(jnp.int32, (q, GROUP_W), 0)
    y = jnp.zeros((q, GROUP_W), f32)
    for j in range(q):
        wj = jnp.where(rowx >= j, jnp.exp(acum_x - acum_x[j:j + 1, :]), 0.0)
        wj = wj * jnp.broadcast_to(cb[:, j:j + 1], (q, GROUP_W))
        y = y + wj * xdt[j:j + 1, :]

    y_off = lax.dot_general(cm, s_prev.astype(bf16), (((1,), (1,)), ((), ())), preferred_element_type=f32)
    y = y + y_off * jnp.exp(acum_x)
    new_states = lax.dot_general(xdtd, bm, (((0,), (0,)), ((), ())), preferred_element_type=f32)
    return y, xs, s_prev * _decay_rows(jnp.exp(tot_b)) + new_states


def _ssd_sample_kernel(xs_ref, b_ref, c_ref, pxs_ref, pb_ref, pc_ref,
                       cwx_ref, cwb_ref, cwc_ref, cbx_ref, cbb_ref, cbc_ref,
                       dt_ref, dtb_ref, alog_ref, dsk_ref, zs_ref, gout_ref, s0_ref,
                       yn_ref, ssm_ref, xpad, *, q, nseq):
    ys, xss = [], []
    stride = 2 * SUBLANES
    for n in range(nseq):
        r0 = n * stride + SUBLANES
        rows = pl.ds(n * q, q)
        _stage(xpad, r0 - (CONV_K - 1), CONV_K - 1, pxs_ref[n], pb_ref[n], pc_ref[n])
        _stage(xpad, r0, q, xs_ref[rows, :], b_ref[rows, :], c_ref[rows, :])
        y, xs, s_new = _ssd_short_chunk(xpad, r0, (cwx_ref, cwb_ref, cwc_ref), (cbx_ref, cbb_ref, cbc_ref),
                                        dt_ref[0, rows, :], dtb_ref[0], alog_ref[0], s0_ref[n, 0], q)
        ssm_ref[n, 0] = s_new
        ys.append(y)
        xss.append(xs)
    _gate_norm_store(jnp.concatenate(ys, axis=0), jnp.concatenate(xss, axis=0), dsk_ref, zs_ref, gout_ref, yn_ref)


def _group_param(v):
    return v.reshape(N_GROUPS, 1, HEADS_PER_GROUP), v.reshape(N_GROUPS, HEADS_PER_GROUP, 1)


def ssd_branch(xbc, dt_raw, zs, conv_prev, s0, conv_w, conv_b, dt_bias, a_log, d_skip, g_out,
               *, row0, nb, seq, name):
    m = xbc.shape[0]
    q = min(seq, CHUNK)
    nc = seq // q
    nseq = 1 if s0 is None else SAMPLE_SEQS_PER_STEP
    rows = nseq * q
    rb0 = row0 // rows
    dtg = dt_raw.reshape(m, N_GROUPS, HEADS_PER_GROUP).transpose(1, 0, 2)
    dtb, dtbt = _group_param(dt_bias)
    alog, alogt = _group_param(a_log)
    dsk = jnp.repeat(d_skip, HEAD_DIM).reshape(1, D_INNER)
    gout = g_out.reshape(1, D_INNER)
    cb2 = conv_b.reshape(1, CONV_DIM)
    nbg, ncg = D_INNER // D_STATE, (D_INNER + N_GROUPS * D_STATE) // D_STATE

    if s0 is None:
        grid = (nb, N_GROUPS, nc)
        ix = lambda f: (lambda b, g, c: f(b, g, rb0 + b * nc + c))
        sem = ("parallel", "parallel", "arbitrary")
    else:
        assert nc == 1 and nb % nseq == 0
        grid = (nb // nseq, N_GROUPS)
        ix = lambda f: (lambda b, g: f(b, g, rb0 + b))
        sem = ("parallel", "parallel")

    in_specs = [
        pl.BlockSpec((rows, GROUP_W), ix(lambda b, g, r: (r, g))),
        pl.BlockSpec((rows, D_STATE), ix(lambda b, g, r: (r, nbg + g))),
        pl.BlockSpec((rows, D_STATE), ix(lambda b, g, r: (r, ncg + g))),
        pl.BlockSpec((nseq, CONV_K - 1, GROUP_W), ix(lambda b, g, r: (b, 0, g))),
        pl.BlockSpec((nseq, CONV_K - 1, D_STATE), ix(lambda b, g, r: (b, 0, nbg + g))),
        pl.BlockSpec((nseq, CONV_K - 1, D_STATE), ix(lambda b, g, r: (b, 0, ncg + g))),
        pl.BlockSpec((CONV_K, GROUP_W), ix(lambda b, g, r: (0, g))),
        pl.BlockSpec((CONV_K, D_STATE), ix(lambda b, g, r: (0, nbg + g))),
        pl.BlockSpec((CONV_K, D_STATE), ix(lambda b, g, r: (0, ncg + g))),
        pl.BlockSpec((1, GROUP_W), ix(lambda b, g, r: (0, g))),
        pl.BlockSpec((1, D_STATE), ix(lambda b, g, r: (0, nbg + g))),
        pl.BlockSpec((1, D_STATE), ix(lambda b, g, r: (0, ncg + g))),
        pl.BlockSpec((1, rows, HEADS_PER_GROUP), ix(lambda b, g, r: (g, r, 0))),
    ]
    args = [xbc, xbc, xbc, conv_prev, conv_prev, conv_prev, conv_w, conv_w, conv_w, cb2, cb2, cb2, dtg]
    vec16 = pl.BlockSpec((1, 1, HEADS_PER_GROUP), ix(lambda b, g, r: (g, 0, 0)))
    vec16t = pl.BlockSpec((1, HEADS_PER_GROUP, 1), ix(lambda b, g, r: (g, 0, 0)))
    tail_specs = [
        pl.BlockSpec((1, GROUP_W), ix(lambda b, g, r: (0, g))),
        pl.BlockSpec((rows, GROUP_W), ix(lambda b, g, r: (r, g))),
        pl.BlockSpec((1, GROUP_W), ix(lambda b, g, r: (0, g))),
    ]
    state_spec = pl.BlockSpec((nseq, 1, GROUP_W, D_STATE), ix(lambda b, g, r: (b, g, 0, 0)))
    yn_spec = pl.BlockSpec((rows, GROUP_W), ix(lambda b, g, r: (r - rb0, g)))
    out_shape = (jax.ShapeDtypeStruct((nb * seq, D_INNER), bf16),
                 jax.ShapeDtypeStruct((nb, N_GROUPS, GROUP_W, D_STATE), f32))

    if s0 is None:
        dtgt = dt_raw.reshape(m, N_GROUPS, HEADS_PER_GROUP).transpose(1, 2, 0)
        in_specs += [pl.BlockSpec((1, HEADS_PER_GROUP, q), ix(lambda b, g, r: (g, 0, r))),
                     vec16, vec16t, vec16, vec16t] + tail_specs
        args += [dtgt, dtb, dtbt, alog, alogt, dsk, zs, gout]
        kern = functools.partial(_ssd_prompt_kernel, nc=nc)
        scratch = [pltpu.VMEM((SUBLANES + q, XBC_W), f32), pltpu.VMEM((GROUP_W, D_STATE), f32)]
    else:
        in_specs += [vec16, vec16] + tail_specs + [state_spec]
        args += [dtb, alog, dsk, zs, gout, s0.reshape(nb, N_GROUPS, GROUP_W, D_STATE)]
        kern = functools.partial(_ssd_sample_kernel, q=q, nseq=nseq)
        scratch = [pltpu.VMEM((nseq * 2 * SUBLANES, XBC_W), f32)]

    yn, ssm = pl.pallas_call(
        kern, grid=grid, in_specs=in_specs, out_specs=(yn_spec, state_spec), out_shape=out_shape,
        scratch_shapes=scratch, compiler_params=_params(sem), name=name,
    )(*args)
    return yn, ssm.reshape(nb, N_HEADS, HEAD_DIM, D_STATE)


def _pool_kernel(u_ref, prev_ref, o_ref, halo, *, tl, n_past):
    lt = pl.program_id(1)
    base = 2 * SUBLANES

    @pl.when(lt == 0)
    def _():
        halo[pl.ds(base - POOL_KEEP, POOL_KEEP), :] = prev_ref[0]

    @pl.when(lt > 0)
    def _():
        halo[pl.ds(base - POOL_KEEP, POOL_KEEP), :] = halo[pl.ds(base + tl - POOL_KEEP, POOL_KEEP), :]

    halo[pl.ds(base, tl), :] = u_ref[...]
    t = lt * tl + lax.broadcasted_iota(jnp.int32, (tl, 1), 0) + (n_past + 1)
    for gi, w in enumerate(POOL_WINDOWS):
        cols = slice(gi * POOL_GROUP, (gi + 1) * POOL_GROUP)
        cur = halo[pl.ds(base, tl), cols]
        acc = cur
        for k in range(1, w):
            acc = acc + halo[pl.ds(base - k, tl), cols]
        cnt = jnp.minimum(t, w).astype(f32)
        o_ref[:, cols] = (acc / cnt - cur).astype(o_ref.dtype)


def pool_branch(u, prev, *, row0, nb, seq, n_past, tl, name):
    d = u.shape[1]
    nlt = seq // tl
    rb0 = row0 // tl
    return pl.pallas_call(
        functools.partial(_pool_kernel, tl=tl, n_past=n_past),
        grid=(nb, nlt),
        in_specs=[pl.BlockSpec((tl, d), lambda b, l: (rb0 + b * nlt + l, 0)),
                  pl.BlockSpec((1, POOL_KEEP, d), lambda b, l: (b, 0, 0))],
        out_specs=pl.BlockSpec((tl, d), lambda b, l: (b * nlt + l, 0)),
        out_shape=jax.ShapeDtypeStruct((nb * seq, d), bf16),
        scratch_shapes=[pltpu.VMEM((2 * SUBLANES + tl, d), f32)],
        compiler_params=_params(("parallel", "arbitrary")),
        name=name,
    )(u, prev)


def _poolmix_kernel(ap_ref, as_ref, w_ref, s_ref, o_ref, *, n_p):
    def mix(a_ref):
        acc = jnp.dot(a_ref[...], w_ref[0].astype(bf16), preferred_element_type=f32)
        o_ref[...] = (acc * s_ref[...]).astype(o_ref.dtype)

    i = pl.program_id(0)
    pl.when(i < n_p)(lambda: mix(ap_ref))
    pl.when(i >= n_p)(lambda: mix(as_ref))


def pool_mix(pooled_p, pooled_s, w_mix, scale, *, tm, name):
    (mp, d), ms = pooled_p.shape, pooled_s.shape[0]
    n_p = mp // tm
    ng = d // POOL_GROUP
    return pl.pallas_call(
        functools.partial(_poolmix_kernel, n_p=n_p),
        grid=((mp + ms) // tm, ng),
        in_specs=[pl.BlockSpec((tm, POOL_GROUP), lambda i, g: (jnp.minimum(i, n_p - 1), g)),
                  pl.BlockSpec((tm, POOL_GROUP), lambda i, g: (jnp.maximum(i - n_p, 0), g)),
                  pl.BlockSpec((1, POOL_GROUP, POOL_GROUP), lambda i, g: (g, 0, 0)),
                  pl.BlockSpec((1, POOL_GROUP), lambda i, g: (0, g))],
        out_specs=pl.BlockSpec((tm, POOL_GROUP), lambda i, g: (i, g)),
        out_shape=jax.ShapeDtypeStruct((mp + ms, d), bf16),
        compiler_params=_params(("arbitrary", "arbitrary")),
        name=name,
    )(pooled_p, pooled_s, w_mix, scale.reshape(1, d))


def _ple_epilogue(acc, x2, p, wp):
    pe = jnp.dot(p, wp.astype(bf16), preferred_element_type=f32)
    return x2 + jax.nn.sigmoid(acc) * pe


def _last_rows(a2d, seq, keep, row0, nb):
    grp = 2 * SUBLANES
    a3 = a2d.reshape(a2d.shape[0] // grp, grp, a2d.shape[1])
    first = row0 // grp + seq // grp - 1
    last16 = lax.slice(a3, (first, 0, 0), (first + (nb - 1) * (seq // grp) + 1, grp, a2d.shape[1]),
                       (seq // grp, 1, 1))
    return last16[:, grp - keep:]


def kernel(x_prompt, x_sample, p_prompt, p_sample, state_ssm, state_conv, state_pool, g_mix, w_in, conv_w, conv_b, dt_bias, a_log, d_skip, g_ssd_out, w_ssd_out, w_pool_mix, pool_scale, w_pool_out, w_o, g_ffn, w_up, w_down, g_ple, w_ple_gate, w_ple_proj, g_final):
    bp, lp, d = x_prompt.shape
    bs, ls, _ = x_sample.shape
    mp, ms = bp * lp, bs * ls
    n_p = mp // TM
    xp2, xs2 = x_prompt.reshape(mp, d), x_sample.reshape(ms, d)
    pe = jnp.concatenate([p_prompt[0].reshape(mp, D_PLE), p_sample[0].reshape(ms, D_PLE)], axis=0).astype(bf16)

    c_xbc = D_INNER
    c_dt = c_xbc + CONV_DIM
    c_u = c_dt + N_HEADS
    c_gate = c_u + D_MODEL
    w_in0 = w_in[0]

    h = rmsnorm_stacked(xp2, xs2, g_mix[0], bf16, name="norm_mix")
    zs = matmul(h, w_in0, n=D_INNER, col0=0, out_dtype=bf16, epilogue=_silu, name="proj_z")
    xbc = matmul(h, w_in0, n=CONV_DIM, col0=c_xbc, name="proj_xbc")
    dt_raw = matmul(h, w_in0, n=N_HEADS, col0=c_dt, tn=N_HEADS, name="proj_dt")
    u = matmul(h, w_in0, n=D_MODEL, col0=c_u, name="proj_u")
    sg = matmul(h, w_in0, n=2 * D_MODEL, col0=c_gate, out_dtype=bf16, epilogue=jax.nn.sigmoid, name="proj_gates")

    ssd_w = (conv_w[0], conv_b[0], dt_bias[0], a_log[0], d_skip[0], g_ssd_out[0])
    conv0_p = jnp.zeros((bp, CONV_K - 1, CONV_DIM), f32)
    yn_p, ssm_p = ssd_branch(xbc, dt_raw, zs, conv0_p, None, *ssd_w, row0=0, nb=bp, seq=lp, name="ssd_prompt")
    yn_s, ssm_s = ssd_branch(xbc, dt_raw, zs, state_conv[0], state_ssm[0], *ssd_w, row0=mp, nb=bs, seq=ls,
                             name="ssd_sample")
    yn = jnp.concatenate([yn_p, yn_s], axis=0)

    pool0_p = jnp.zeros((bp, POOL_KEEP, d), f32)
    pooled_p = pool_branch(u, pool0_p, row0=0, nb=bp, seq=lp, n_past=0, tl=256, name="pool_prompt")
    pooled_s = pool_branch(u, state_pool[0], row0=mp, nb=bs, seq=ls, n_past=PAST_LEN, tl=ls, name="pool_sample")
    pm = pool_mix(pooled_p, pooled_s, w_pool_mix[0], pool_scale[0], tm=TM, name="pool_mix")

    ta = matmul(yn, w_ssd_out[0], tk=D_INNER // 2, name="ssd_out",
                epilogue=lambda acc, ga: ga.astype(f32) * acc, extras=[_tile(sg)])
    merged = matmul(pm, w_pool_out[0], out_dtype=bf16, name="pool_out_merge",
                    epilogue=lambda acc, gb, t: t + gb.astype(f32) * acc,
                    extras=[(sg, (TM, TN), lambda i, j, k: (i, j + D_MODEL // TN)), _tile(ta)])
    x1 = matmul(merged, w_o[0], name="out_proj",
                epilogue=lambda acc, r_p, r_s: _pick_stacked(r_p, r_s, n_p) + acc,
                extras=_stacked_tiles(xp2, xs2))

    h2 = rmsnorm(x1, g_ffn[0], bf16, name="norm_ffn")
    act = matmul(h2, w_up[0], out_dtype=bf16, name="mlp_up",
                 epilogue=lambda acc: jnp.square(jnp.maximum(acc, 0.0)))
    x2 = matmul(act, w_down[0], tk=D_MODEL, name="mlp_down",
                epilogue=lambda acc, r: r + acc, extras=[_tile(x1)])

    h3 = rmsnorm(x2, g_ple[0], bf16, name="norm_ple")
    x3 = matmul(h3, w_ple_gate[0], name="ple", epilogue=_ple_epilogue,
                extras=[_tile(x2),
                        (pe, (TM, D_PLE), lambda i, j, k: (i, 0)),
                        (w_ple_proj[0], (D_PLE, TN), lambda i, j, k: (0, j))])

    y_p = rmsnorm(x3, g_final, f32, rows=mp, name="norm_final_prompt").reshape(bp, lp, d)
    y_s = rmsnorm(x3, g_final, f32, row_block0=mp // 512, rows=ms, name="norm_final_sample").reshape(bs, ls, d)

    conv_p = _last_rows(xbc, lp, CONV_K - 1, 0, bp)
    pool_p = _last_rows(u, lp, POOL_KEEP, 0, bp)
    xbc_s = xbc.reshape(-1, SUBLANES, CONV_DIM)[mp // SUBLANES:]
    u_s = u.reshape(-1, SUBLANES, d)[mp // SUBLANES:]
    conv_s = xbc_s[:, ls - (CONV_K - 1):]
    pool_s = jnp.concatenate([state_pool[0][:, ls:], u_s], axis=1)
    return (y_p, y_s, ssm_p[None], conv_p[None], pool_p[None], ssm_s[None], conv_s[None], pool_s[None])
```

```python
import functools

import jax
import jax.numpy as jnp
from jax import lax
from jax.experimental import pallas as pl
from jax.experimental.pallas import tpu as pltpu

f32 = jnp.float32
bf16 = jnp.bfloat16

D_MODEL = 4096
D_INNER = 8192
HEAD_DIM = 64
N_HEADS = 128
N_GROUPS = 8
HEADS_PER_GROUP = N_HEADS // N_GROUPS
GROUP_W = D_INNER // N_GROUPS
D_STATE = 128
CONV_K = 4
CONV_DIM = D_INNER + 2 * N_GROUPS * D_STATE
XBC_W = GROUP_W + 2 * D_STATE
POOL_WINDOWS = (2, 4, 8, 16)
POOL_GROUP = 1024
POOL_KEEP = 15
D_PLE = 256
EPS = 1e-6
SSD_NORM_EPS = 1e-5
CHUNK = 128
PAST_LEN = 16384

LANES = 128
SUBLANES = 8
VMEM_LIMIT = 56 * 1024 * 1024
TM = 1024
TM_BIG = 1536
TN = 512
KT = dict(tm=TM_BIG, tn=1024, tk=1024)
SAMPLE_SEQS_PER_STEP = 8


def _params(sem):
    return pltpu.CompilerParams(dimension_semantics=sem, vmem_limit_bytes=VMEM_LIMIT)


def _rms(x, g, eps):
    ms = jnp.mean(x * x, axis=-1, keepdims=True)
    return x * lax.rsqrt(ms + eps) * g


def _rmsnorm_kernel(x_ref, g_ref, o_ref, *, eps):
    o_ref[...] = _rms(x_ref[...], g_ref[...], eps).astype(o_ref.dtype)


def rmsnorm(x, g, out_dtype, *, name, tm=512, row_block0=0, rows=None):
    m, d = x.shape
    rows = m if rows is None else rows
    return pl.pallas_call(
        functools.partial(_rmsnorm_kernel, eps=EPS),
        grid=(rows // tm,),
        in_specs=[pl.BlockSpec((tm, d), lambda i: (i + row_block0, 0)),
                  pl.BlockSpec((1, d), lambda i: (0, 0))],
        out_specs=pl.BlockSpec((tm, d), lambda i: (i, 0)),
        out_shape=jax.ShapeDtypeStruct((rows, d), out_dtype),
        compiler_params=_params(("parallel",)),
        name=name,
    )(x, g.reshape(1, d))


def _rmsnorm2_kernel(xp_ref, xs_ref, g_ref, o_ref, *, eps, n_p):
    i = pl.program_id(0)

    @pl.when(i < n_p)
    def _():
        o_ref[...] = _rms(xp_ref[...], g_ref[...], eps).astype(o_ref.dtype)

    @pl.when(i >= n_p)
    def _():
        o_ref[...] = _rms(xs_ref[...], g_ref[...], eps).astype(o_ref.dtype)


def rmsnorm_stacked(x_p, x_s, g, out_dtype, *, name, tm=512):
    (mp, d), ms = x_p.shape, x_s.shape[0]
    n_p = mp // tm
    return pl.pallas_call(
        functools.partial(_rmsnorm2_kernel, eps=EPS, n_p=n_p),
        grid=((mp + ms) // tm,),
        in_specs=[pl.BlockSpec((tm, d), lambda i: (jnp.minimum(i, n_p - 1), 0)),
                  pl.BlockSpec((tm, d), lambda i: (jnp.maximum(i - n_p, 0), 0)),
                  pl.BlockSpec((1, d), lambda i: (0, 0))],
        out_specs=pl.BlockSpec((tm, d), lambda i: (i, 0)),
        out_shape=jax.ShapeDtypeStruct((mp + ms, d), out_dtype),
        compiler_params=_params(("arbitrary",)),
        name=name,
    )(x_p, x_s, g.reshape(1, d))


def _mm_kernel(*refs, nk, n_extra, epilogue):
    a_ref, w_ref = refs[0], refs[1]
    extras = refs[2:2 + n_extra]
    o_ref = refs[2 + n_extra]
    def product():
        return jnp.dot(a_ref[...], w_ref[...].astype(bf16), preferred_element_type=f32)

    if nk == 1:
        o_ref[...] = epilogue(product(), *[e[...] for e in extras]).astype(o_ref.dtype)
    else:
        k = pl.program_id(2)

        @pl.when(k == 0)
        def _():
            o_ref[...] = extras[0][...] + product() if n_extra else product()

        @pl.when(k > 0)
        def _():
            o_ref[...] += product()


def matmul(a, w, *, name, n=None, col0=0, tm=TM, tn=TN, tk=None, out_dtype=f32, epilogue=None, extras=()):
    m, kdim = a.shape
    n = w.shape[1] if n is None else n
    tk = kdim if tk is None else tk
    nk = kdim // tk
    if nk > 1:
        assert epilogue is None and out_dtype == f32 and len(extras) <= 1
    if epilogue is None:
        epilogue = lambda acc: acc
    if col0 % tn == 0:
        w_spec = pl.BlockSpec((tk, tn), lambda i, j, k: (k, col0 // tn + j))
    else:
        w_spec = pl.BlockSpec((pl.Element(tk), pl.Element(tn)),
                              lambda i, j, k: (k * tk, pl.multiple_of(col0 + j * tn, LANES)))
    in_specs = [pl.BlockSpec((tm, tk), lambda i, j, k: (i, k)), w_spec]
    in_specs += [pl.BlockSpec(blk, imap) for (_, blk, imap) in extras]
    return pl.pallas_call(
        functools.partial(_mm_kernel, nk=nk, n_extra=len(extras), epilogue=epilogue),
        grid=(m // tm, n // tn, nk),
        in_specs=in_specs,
        out_specs=pl.BlockSpec((tm, tn), lambda i, j, k: (i, j)),
        out_shape=jax.ShapeDtypeStruct((m, n), out_dtype),
        compiler_params=_params(("parallel", "parallel", "arbitrary")),
        name=name,
    )(a, w, *[e[0] for e in extras])


def _tile(arr, tm=TM, tn=TN):
    return (arr, (tm, tn), lambda i, j, k: (i, j))


def _stacked_tiles(x_p, x_s, tm=TM, tn=TN):
    n_p = x_p.shape[0] // tm
    return [(x_p, (tm, tn), lambda i, j, k: (jnp.minimum(i, n_p - 1), j)),
            (x_s, (tm, tn), lambda i, j, k: (jnp.maximum(i - n_p, 0), j))]


def _pick_stacked(t_p, t_s, n_p):
    return jnp.where(pl.program_id(0) < n_p, t_p, t_s)


def _silu(x):
    return x * jax.nn.sigmoid(x)


def _softplus(x):
    return jnp.maximum(x, 0.0) + jnp.log1p(jnp.exp(-jnp.abs(x)))


def _expand_heads(x, rows):
    lane = lax.broadcasted_iota(jnp.int32, (rows, LANES), 1)
    tiles = []
    for k in range(HEADS_PER_GROUP // 2):
        a = jnp.broadcast_to(x[:, 2 * k:2 * k + 1], (rows, LANES))
        b = jnp.broadcast_to(x[:, 2 * k + 1:2 * k + 2], (rows, LANES))
        tiles.append(jnp.where(lane < HEAD_DIM, a, b))
    return jnp.concatenate(tiles, axis=1)


def _conv_silu(xpad, r0, cw_refs, cb_refs, q):
    w = jnp.concatenate([r[...] for r in cw_refs], axis=1)
    bias = jnp.concatenate([r[...] for r in cb_refs], axis=1)
    x = xpad[pl.ds(r0 - SUBLANES, q + SUBLANES), :]
    acc = w[0:1, :] * x
    for k in range(1, CONV_K):
        acc = pltpu.roll(acc, 1, axis=0) + w[k:k + 1, :] * x
    return _silu(acc[SUBLANES:, :] + bias)


def _stage(xpad, r0, nrows, xs, b, c):
    xpad[pl.ds(r0, nrows), 0:GROUP_W] = xs
    xpad[pl.ds(r0, nrows), GROUP_W:GROUP_W + D_STATE] = b
    xpad[pl.ds(r0, nrows), GROUP_W + D_STATE:XBC_W] = c


def _stage_halo(xpad, r0, pxs, pb, pc):
    xpad[pl.ds(r0 - SUBLANES, SUBLANES), :] = jnp.zeros((SUBLANES, XBC_W), f32)
    _stage(xpad, r0 - (CONV_K - 1), CONV_K - 1, pxs, pb, pc)


def _gate_norm_store(y, xs, dsk_ref, zs_ref, gout_ref, yn_ref):
    y = y + dsk_ref[...] * xs
    y = y * zs_ref[...].astype(f32)
    ms = jnp.mean(y * y, axis=-1, keepdims=True)
    yn_ref[...] = (y * lax.rsqrt(ms + SSD_NORM_EPS) * gout_ref[...]).astype(yn_ref.dtype)


def _decay_rows(cd_b):
    return jnp.concatenate(
        [jnp.broadcast_to(cd_b[h:h + 1, :], (HEAD_DIM, D_STATE)) for h in range(HEADS_PER_GROUP)], axis=0)


def _ssd_prompt_kernel(xs_ref, b_ref, c_ref, pxs_ref, pb_ref, pc_ref,
                       cwx_ref, cwb_ref, cwc_ref, cbx_ref, cbb_ref, cbc_ref,
                       dt_ref, dtt_ref, dtb_ref, dtbt_ref, alog_ref, alogt_ref,
                       dsk_ref, zs_ref, gout_ref,
                       yn_ref, ssm_ref, xpad, state, *, nc):
    q = CHUNK
    r0 = SUBLANES
    c = pl.program_id(2)
    hi = lax.Precision.HIGHEST

    @pl.when(c == 0)
    def _():
        state[...] = jnp.zeros_like(state)
        _stage_halo(xpad, r0, pxs_ref[0], pb_ref[0], pc_ref[0])

    @pl.when(c > 0)
    def _():
        xpad[pl.ds(r0 - 3, 3), :] = xpad[pl.ds(r0 + q - 3, 3), :]

    _stage(xpad, r0, q, xs_ref[...], b_ref[...], c_ref[...])
    act = _conv_silu(xpad, r0, (cwx_ref, cwb_ref, cwc_ref), (cbx_ref, cbb_ref, cbc_ref), q)
    xs = act[:, 0:GROUP_W]
    bm = act[:, GROUP_W:GROUP_W + D_STATE].astype(bf16)
    cm = act[:, GROUP_W + D_STATE:XBC_W].astype(bf16)

    dt = _softplus(dt_ref[0] + dtb_ref[0])
    dta = dt * (-jnp.exp(alog_ref[0]))
    dtt = _softplus(dtt_ref[0] + dtbt_ref[0])
    dtat = dtt * (-jnp.exp(alogt_ref[0]))
    row = lax.broadcasted_iota(jnp.int32, (q, q), 0)
    col = lax.broadcasted_iota(jnp.int32, (q, q), 1)
    causal = row >= col
    acum = jnp.dot(causal.astype(f32), dta, precision=hi, preferred_element_type=f32)
    acum_t = jnp.dot(dtat, (row <= col).astype(f32), precision=hi, preferred_element_type=f32)
    tot_b = jnp.dot(dtat, jnp.ones((q, D_STATE), f32), precision=hi, preferred_element_type=f32)

    dt_x = _expand_heads(dt, q)
    acum_x = _expand_heads(acum, q)
    xdt = xs * dt_x
    decay_x = jnp.exp(acum_x[q - 1:q, :] - acum_x)
    xdtd = (xdt * decay_x).astype(bf16)
    xdt_b = xdt.astype(bf16)

    cb = lax.dot_general(cm, bm, (((1,), (1,)), ((), ())), preferred_element_type=f32)
    cb = jnp.where(causal, cb, 0.0)
    lane = lax.broadcasted_iota(jnp.int32, (q, LANES), 1)
    tiles = []
    for k in range(HEADS_PER_GROUP // 2):
        res = []
        for h in (2 * k, 2 * k + 1):
            seg = jnp.broadcast_to(acum[:, h:h + 1], (q, q)) - jnp.broadcast_to(acum_t[h:h + 1, :], (q, q))
            m_h = (cb * jnp.exp(jnp.minimum(seg, 0.0))).astype(bf16)
            res.append(jnp.dot(m_h, xdt_b[:, k * LANES:(k + 1) * LANES], preferred_element_type=f32))
        tiles.append(jnp.where(lane < HEAD_DIM, res[0], res[1]))
    y_diag = jnp.concatenate(tiles, axis=1)

    s_prev = state[...]
    y_off = lax.dot_general(cm, s_prev.astype(bf16), (((1,), (1,)), ((), ())), preferred_element_type=f32)
    y_off = y_off * jnp.exp(acum_x)
    new_states = lax.dot_general(xdtd, bm, (((0,), (0,)), ((), ())), preferred_element_type=f32)
    s_new = s_prev * _decay_rows(jnp.exp(tot_b)) + new_states
    state[...] = s_new

    @pl.when(c == nc - 1)
    def _():
        ssm_ref[0, 0] = s_new

    _gate_norm_store(y_diag + y_off, xs, dsk_ref, zs_ref, gout_ref, yn_ref)


def _ssd_short_chunk(xpad, r0, cw_refs, cb_refs, dt_raw, dtb, alog, s_prev, q):
    hi = lax.Precision.HIGHEST
    act = _conv_silu(xpad, r0, cw_refs, cb_refs, q)
    xs = act[:, 0:GROUP_W]
    bm = act[:, GROUP_W:GROUP_W + D_STATE].astype(bf16)
    cm = act[:, GROUP_W + D_STATE:XBC_W].astype(bf16)

    dt = _softplus(dt_raw + dtb)
    dta = dt * (-jnp.exp(alog))
    row = lax.broadcasted_iota(jnp.int32, (q, q), 0)
    col = lax.broadcasted_iota(jnp.int32, (q, q), 1)
    acum = jnp.dot((row >= col).astype(f32), dta, precision=hi, preferred_element_type=f32)
    tot_b = lax.dot_general(dta, jnp.ones((q, D_STATE), f32), (((0,), (0,)), ((), ())),
                            precision=hi, preferred_element_type=f32)

    dt_x = _expand_heads(dt, q)
    acum_x = _expand_heads(acum, q)
    xdt = xs * dt_x
    decay_x = jnp.exp(acum_x[q - 1:q, :] - acum_x)
    xdtd = (xdt * decay_x).astype(bf16)

    cb = lax.dot_general(cm, bm, (((1,), (1,)), ((), ())), preferred_element_type=f32)
    rowx = lax.broadcasted_iota(jnp.int32, (q, GROUP_W), 0)
    y = jnp.zeros((q, GROUP_W), f32)
    for j in range(q):
        wj = jnp.where(rowx >= j, jnp.exp(acum_x - acum_x[j:j + 1, :]), 0.0)
        wj = wj * jnp.broadcast_to(cb[:, j:j + 1], (q, GROUP_W))
        y = y + wj * xdt[j:j + 1, :]

    y_off = lax.dot_general(cm, s_prev.astype(bf16), (((1,), (1,)), ((), ())), preferred_element_type=f32)
    y = y + y_off * jnp.exp(acum_x)
    new_states = lax.dot_general(xdtd, bm, (((0,), (0,)), ((), ())), preferred_element_type=f32)
    return y, xs, s_prev * _decay_rows(jnp.exp(tot_b)) + new_states


def _ssd_sample_kernel(xs_ref, b_ref, c_ref, pxs_ref, pb_ref, pc_ref,
                       cwx_ref, cwb_ref, cwc_ref, cbx_ref, cbb_ref, cbc_ref,
                       dt_ref, dtb_ref, alog_ref, dsk_ref, zs_ref, gout_ref, s0_ref,
                       yn_ref, ssm_ref, xpad, *, q, nseq):
    ys, xss = [], []
    stride = 2 * SUBLANES
    for n in range(nseq):
        r0 = n * stride + SUBLANES
        rows = pl.ds(n * q, q)
        _stage_halo(xpad, r0, pxs_ref[n], pb_ref[n], pc_ref[n])
        _stage(xpad, r0, q, xs_ref[rows, :], b_ref[rows, :], c_ref[rows, :])
        y, xs, s_new = _ssd_short_chunk(xpad, r0, (cwx_ref, cwb_ref, cwc_ref), (cbx_ref, cbb_ref, cbc_ref),
                                        dt_ref[0, rows, :], dtb_ref[0], alog_ref[0], s0_ref[n, 0], q)
        ssm_ref[n, 0] = s_new
        ys.append(y)
        xss.append(xs)
    _gate_norm_store(jnp.concatenate(ys, axis=0), jnp.concatenate(xss, axis=0), dsk_ref, zs_ref, gout_ref, yn_ref)


def _group_param(v):
    return v.reshape(N_GROUPS, 1, HEADS_PER_GROUP), v.reshape(N_GROUPS, HEADS_PER_GROUP, 1)


def ssd_branch(xbc, dt_raw, zs, conv_prev, s0, conv_w, conv_b, dt_bias, a_log, d_skip, g_out,
               *, row0, nb, seq, name):
    m = xbc.shape[0]
    q = min(seq, CHUNK)
    nc = seq // q
    nseq = 1 if s0 is None else SAMPLE_SEQS_PER_STEP
    rows = nseq * q
    rb0 = row0 // rows
    dtg = dt_raw.reshape(m, N_GROUPS, HEADS_PER_GROUP).transpose(1, 0, 2)
    dtb, dtbt = _group_param(dt_bias)
    alog, alogt = _group_param(a_log)
    dsk = jnp.repeat(d_skip, HEAD_DIM).reshape(1, D_INNER)
    gout = g_out.reshape(1, D_INNER)
    cb2 = conv_b.reshape(1, CONV_DIM)
    nbg, ncg = D_INNER // D_STATE, (D_INNER + N_GROUPS * D_STATE) // D_STATE

    if s0 is None:
        grid = (nb, N_GROUPS, nc)
        ix = lambda f: (lambda b, g, c: f(b, g, rb0 + b * nc + c))
        sem = ("parallel", "parallel", "arbitrary")
    else:
        assert nc == 1 and nb % nseq == 0
        grid = (nb // nseq, N_GROUPS)
        ix = lambda f: (lambda b, g: f(b, g, rb0 + b))
        sem = ("parallel", "parallel")

    in_specs = [
        pl.BlockSpec((rows, GROUP_W), ix(lambda b, g, r: (r, g))),
        pl.BlockSpec((rows, D_STATE), ix(lambda b, g, r: (r, nbg + g))),
        pl.BlockSpec((rows, D_STATE), ix(lambda b, g, r: (r, ncg + g))),
        pl.BlockSpec((nseq, CONV_K - 1, GROUP_W), ix(lambda b, g, r: (b, 0, g))),
        pl.BlockSpec((nseq, CONV_K - 1, D_STATE), ix(lambda b, g, r: (b, 0, nbg + g))),
        pl.BlockSpec((nseq, CONV_K - 1, D_STATE), ix(lambda b, g, r: (b, 0, ncg + g))),
        pl.BlockSpec((CONV_K, GROUP_W), ix(lambda b, g, r: (0, g))),
        pl.BlockSpec((CONV_K, D_STATE), ix(lambda b, g, r: (0, nbg + g))),
        pl.BlockSpec((CONV_K, D_STATE), ix(lambda b, g, r: (0, ncg + g))),
        pl.BlockSpec((1, GROUP_W), ix(lambda b, g, r: (0, g))),
        pl.BlockSpec((1, D_STATE), ix(lambda b, g, r: (0, nbg + g))),
        pl.BlockSpec((1, D_STATE), ix(lambda b, g, r: (0, ncg + g))),
        pl.BlockSpec((1, rows, HEADS_PER_GROUP), ix(lambda b, g, r: (g, r, 0))),
    ]
    args = [xbc, xbc, xbc, conv_prev, conv_prev, conv_prev, conv_w, conv_w, conv_w, cb2, cb2, cb2, dtg]
    vec16 = pl.BlockSpec((1, 1, HEADS_PER_GROUP), ix(lambda b, g, r: (g, 0, 0)))
    vec16t = pl.BlockSpec((1, HEADS_PER_GROUP, 1), ix(lambda b, g, r: (g, 0, 0)))
    tail_specs = [
        pl.BlockSpec((1, GROUP_W), ix(lambda b, g, r: (0, g))),
        pl.BlockSpec((rows, GROUP_W), ix(lambda b, g, r: (r, g))),
        pl.BlockSpec((1, GROUP_W), ix(lambda b, g, r: (0, g))),
    ]
    state_spec = pl.BlockSpec((nseq, 1, GROUP_W, D_STATE), ix(lambda b, g, r: (b, g, 0, 0)))
    yn_spec = pl.BlockSpec((rows, GROUP_W), ix(lambda b, g, r: (r - rb0, g)))
    out_shape = (jax.ShapeDtypeStruct((nb * seq, D_INNER), bf16),
                 jax.ShapeDtypeStruct((nb, N_GROUPS, GROUP_W, D_STATE), f32))

    if s0 is None:
        dtgt = dt_raw.reshape(m, N_GROUPS, HEADS_PER_GROUP).transpose(1, 2, 0)
        in_specs += [pl.BlockSpec((1, HEADS_PER_GROUP, q), ix(lambda b, g, r: (g, 0, r))),
                     vec16, vec16t, vec16, vec16t] + tail_specs
        args += [dtgt, dtb, dtbt, alog, alogt, dsk, zs, gout]
        kern = functools.partial(_ssd_prompt_kernel, nc=nc)
        scratch = [pltpu.VMEM((SUBLANES + q, XBC_W), f32), pltpu.VMEM((GROUP_W, D_STATE), f32)]
    else:
        in_specs += [vec16, vec16] + tail_specs + [state_spec]
        args += [dtb, alog, dsk, zs, gout, s0.reshape(nb, N_GROUPS, GROUP_W, D_STATE)]
        kern = functools.partial(_ssd_sample_kernel, q=q, nseq=nseq)
        scratch = [pltpu.VMEM((nseq * 2 * SUBLANES, XBC_W), f32)]

    yn, ssm = pl.pallas_call(
        kern, grid=grid, in_specs=in_specs, out_specs=(yn_spec, state_spec), out_shape=out_shape,
        scratch_shapes=scratch, compiler_params=_params(sem), name=name,
    )(*args)
    return yn, ssm.reshape(nb, N_HEADS, HEAD_DIM, D_STATE)


def _pool_kernel(u_ref, prev_ref, o_ref, halo, *, tl, n_past):
    lt = pl.program_id(1)
    base = 2 * SUBLANES

    @pl.when(lt == 0)
    def _():
        halo[pl.ds(base - POOL_KEEP, POOL_KEEP), :] = prev_ref[0]

    @pl.when(lt > 0)
    def _():
        halo[pl.ds(base - POOL_KEEP, POOL_KEEP), :] = halo[pl.ds(base + tl - POOL_KEEP, POOL_KEEP), :]

    halo[pl.ds(base, tl), :] = u_ref[...]
    t = lt * tl + lax.broadcasted_iota(jnp.int32, (tl, 1), 0) + (n_past + 1)
    for gi, w in enumerate(POOL_WINDOWS):
        cols = slice(gi * POOL_GROUP, (gi + 1) * POOL_GROUP)
        cur = halo[pl.ds(base, tl), cols]
        acc = cur
        for k in range(1, w):
            acc = acc + halo[pl.ds(base - k, tl), cols]
        cnt = jnp.minimum(t, w).astype(f32)
        o_ref[:, cols] = (acc / cnt - cur).astype(o_ref.dtype)


def pool_branch(u, prev, *, row0, nb, seq, n_past, tl, name):
    d = u.shape[1]
    nlt = seq // tl
    rb0 = row0 // tl
    return pl.pallas_call(
        functools.partial(_pool_kernel, tl=tl, n_past=n_past),
        grid=(nb, nlt),
        in_specs=[pl.BlockSpec((tl, d), lambda b, l: (rb0 + b * nlt + l, 0)),
                  pl.BlockSpec((1, POOL_KEEP, d), lambda b, l: (b, 0, 0))],
        out_specs=pl.BlockSpec((tl, d), lambda b, l: (b * nlt + l, 0)),
        out_shape=jax.ShapeDtypeStruct((nb * seq, d), bf16),
        scratch_shapes=[pltpu.VMEM((2 * SUBLANES + tl, d), f32)],
        compiler_params=_params(("parallel", "arbitrary")),
        name=name,
    )(u, prev)


def _poolmix_kernel(ap_ref, as_ref, w_ref, s_ref, o_ref, *, n_p):
    def mix(a_ref):
        acc = jnp.dot(a_ref[...], w_ref[0].astype(bf16), preferred_element_type=f32)
        o_ref[...] = (acc * s_ref[...]).astype(o_ref.dtype)

    i = pl.program_id(0)
    pl.when(i < n_p)(lambda: mix(ap_ref))
    pl.when(i >= n_p)(lambda: mix(as_ref))


def pool_mix(pooled_p, pooled_s, w_mix, scale, *, tm, name):
    (mp, d), ms = pooled_p.shape, pooled_s.shape[0]
    n_p = mp // tm
    ng = d // POOL_GROUP
    return pl.pallas_call(
        functools.partial(_poolmix_kernel, n_p=n_p),
        grid=((mp + ms) // tm, ng),
        in_specs=[pl.BlockSpec((tm, POOL_GROUP), lambda i, g: (jnp.minimum(i, n_p - 1), g)),
                  pl.BlockSpec((tm, POOL_GROUP), lambda i, g: (jnp.maximum(i - n_p, 0), g)),
                  pl.BlockSpec((1, POOL_GROUP, POOL_GROUP), lambda i, g: (g, 0, 0)),
                  pl.BlockSpec((1, POOL_GROUP), lambda i, g: (0, g))],
        out_specs=pl.BlockSpec((tm, POOL_GROUP), lambda i, g: (i, g)),
        out_shape=jax.ShapeDtypeStruct((mp + ms, d), bf16),
        compiler_params=_params(("arbitrary", "arbitrary")),
        name=name,
    )(pooled_p, pooled_s, w_mix, scale.reshape(1, d))


def _ple_epilogue(acc, x2, p, wp):
    pe = jnp.dot(p, wp.astype(bf16), preferred_element_type=f32)
    return x2 + jax.nn.sigmoid(acc) * pe


def _last_rows(a2d, seq, keep, row0, nb):
    grp = 2 * SUBLANES
    a3 = a2d.reshape(a2d.shape[0] // grp, grp, a2d.shape[1])
    first = row0 // grp + seq // grp - 1
    last16 = lax.slice(a3, (first, 0, 0), (first + (nb - 1) * (seq // grp) + 1, grp, a2d.shape[1]),
                       (seq // grp, 1, 1))
    return last16[:, grp - keep:]


def kernel(x_prompt, x_sample, p_prompt, p_sample, state_ssm, state_conv, state_pool, g_mix, w_in, conv_w, conv_b, dt_bias, a_log, d_skip, g_ssd_out, w_ssd_out, w_pool_mix, pool_scale, w_pool_out, w_o, g_ffn, w_up, w_down, g_ple, w_ple_gate, w_ple_proj, g_final):
    bp, lp, d = x_prompt.shape
    bs, ls, _ = x_sample.shape
    mp, ms = bp * lp, bs * ls
    n_p = mp // TM
    xp2, xs2 = x_prompt.reshape(mp, d), x_sample.reshape(ms, d)
    pe = jnp.concatenate([p_prompt[0].reshape(mp, D_PLE), p_sample[0].reshape(ms, D_PLE)], axis=0).astype(bf16)

    c_xbc = D_INNER
    c_dt = c_xbc + CONV_DIM
    c_u = c_dt + N_HEADS
    c_gate = c_u + D_MODEL
    w_in0 = w_in[0]

    h = rmsnorm_stacked(xp2, xs2, g_mix[0], bf16, name="norm_mix")
    zs = matmul(h, w_in0, n=D_INNER, col0=0, tm=TM_BIG, out_dtype=bf16, epilogue=_silu, name="proj_z")
    xbc = matmul(h, w_in0, n=CONV_DIM, col0=c_xbc, tm=TM_BIG, name="proj_xbc")
    dt_raw = matmul(h, w_in0, n=N_HEADS, col0=c_dt, tm=TM_BIG, tn=N_HEADS, name="proj_dt")
    u = matmul(h, w_in0, n=D_MODEL, col0=c_u, tm=TM_BIG, name="proj_u")
    sg = matmul(h, w_in0, n=2 * D_MODEL, col0=c_gate, tm=TM_BIG, out_dtype=bf16, epilogue=jax.nn.sigmoid,
                name="proj_gates")

    ssd_w = (conv_w[0], conv_b[0], dt_bias[0], a_log[0], d_skip[0], g_ssd_out[0])
    conv0_p = jnp.zeros((bp, CONV_K - 1, CONV_DIM), f32)
    yn_p, ssm_p = ssd_branch(xbc, dt_raw, zs, conv0_p, None, *ssd_w, row0=0, nb=bp, seq=lp, name="ssd_prompt")
    yn_s, ssm_s = ssd_branch(xbc, dt_raw, zs, state_conv[0], state_ssm[0], *ssd_w, row0=mp, nb=bs, seq=ls,
                             name="ssd_sample")
    yn = jnp.concatenate([yn_p, yn_s], axis=0)

    pool0_p = jnp.zeros((bp, POOL_KEEP, d), f32)
    pooled_p = pool_branch(u, pool0_p, row0=0, nb=bp, seq=lp, n_past=0, tl=256, name="pool_prompt")
    pooled_s = pool_branch(u, state_pool[0], row0=mp, nb=bs, seq=ls, n_past=PAST_LEN, tl=ls, name="pool_sample")
    pm = pool_mix(pooled_p, pooled_s, w_pool_mix[0], pool_scale[0], tm=TM, name="pool_mix")

    ya = matmul(yn, w_ssd_out[0], **KT, name="ssd_out")
    merged = matmul(pm, w_pool_out[0], out_dtype=bf16, name="pool_out_merge",
                    epilogue=lambda yb, ga, gb, ya_t: ga.astype(f32) * ya_t + gb.astype(f32) * yb,
                    extras=[_tile(sg), (sg, (TM, TN), lambda i, j, k: (i, j + D_MODEL // TN)), _tile(ya)])
    x1 = matmul(merged, w_o[0], name="out_proj",
                epilogue=lambda acc, r_p, r_s: _pick_stacked(r_p, r_s, n_p) + acc,
                extras=_stacked_tiles(xp2, xs2))

    h2 = rmsnorm(x1, g_ffn[0], bf16, name="norm_ffn")
    act = matmul(h2, w_up[0], tm=TM_BIG, out_dtype=bf16, name="mlp_up",
                 epilogue=lambda acc: jnp.square(jnp.maximum(acc, 0.0)))
    x2 = matmul(act, w_down[0], **KT, name="mlp_down", extras=[_tile(x1, KT["tm"], KT["tn"])])

    h3 = rmsnorm(x2, g_ple[0], bf16, name="norm_ple")
    x3 = matmul(h3, w_ple_gate[0], name="ple", epilogue=_ple_epilogue,
                extras=[_tile(x2),
                        (pe, (TM, D_PLE), lambda i, j, k: (i, 0)),
                        (w_ple_proj[0], (D_PLE, TN), lambda i, j, k: (0, j))])

    y_p = rmsnorm(x3, g_final, f32, rows=mp, name="norm_final_prompt").reshape(bp, lp, d)
    y_s = rmsnorm(x3, g_final, f32, row_block0=mp // 512, rows=ms, name="norm_final_sample").reshape(bs, ls, d)

    conv_p = _last_rows(xbc, lp, CONV_K - 1, 0, bp)
    pool_p = _last_rows(u, lp, POOL_KEEP, 0, bp)
    xbc_s = xbc.reshape(-1, SUBLANES, CONV_DIM)[mp // SUBLANES:]
    u_s = u.reshape(-1, SUBLANES, d)[mp // SUBLANES:]
    conv_s = xbc_s[:, ls - (CONV_K - 1):]
    pool_s = jnp.concatenate([state_pool[0][:, ls:], u_s], axis=1)
    return (y_p, y_s, ssm_p[None], conv_p[None], pool_p[None], ssm_s[None], conv_s[None], pool_s[None])
```

```python
import functools

import jax
import jax.numpy as jnp
from jax import lax
from jax.experimental import pallas as pl
from jax.experimental.pallas import tpu as pltpu

f32 = jnp.float32
bf16 = jnp.bfloat16

D_MODEL = 4096
D_INNER = 8192
HEAD_DIM = 64
N_HEADS = 128
N_GROUPS = 8
HEADS_PER_GROUP = N_HEADS // N_GROUPS
GROUP_W = D_INNER // N_GROUPS
D_STATE = 128
CONV_K = 4
CONV_DIM = D_INNER + 2 * N_GROUPS * D_STATE
XBC_W = GROUP_W + 2 * D_STATE
POOL_WINDOWS = (2, 4, 8, 16)
POOL_GROUP = 1024
POOL_KEEP = 15
D_PLE = 256
EPS = 1e-6
SSD_NORM_EPS = 1e-5
CHUNK = 128
PAST_LEN = 16384

LANES = 128
SUBLANES = 8
VMEM_LIMIT = 56 * 1024 * 1024
TM = 1024
TM_BIG = 1536
TN = 512
KT = dict(tm=TM_BIG, tn=1024, tk=1024)
SAMPLE_SEQS_PER_STEP = 8


def _params(sem):
    return pltpu.CompilerParams(dimension_semantics=sem, vmem_limit_bytes=VMEM_LIMIT)


def _rms(x, g, eps):
    ms = jnp.mean(x * x, axis=-1, keepdims=True)
    return x * lax.rsqrt(ms + eps) * g


def _rmsnorm_kernel(x_ref, g_ref, o_ref, *, eps):
    o_ref[...] = _rms(x_ref[...], g_ref[...], eps).astype(o_ref.dtype)


def rmsnorm(x, g, out_dtype, *, name, tm=512, row_block0=0, rows=None):
    m, d = x.shape
    rows = m if rows is None else rows
    return pl.pallas_call(
        functools.partial(_rmsnorm_kernel, eps=EPS),
        grid=(rows // tm,),
        in_specs=[pl.BlockSpec((tm, d), lambda i: (i + row_block0, 0)),
                  pl.BlockSpec((1, d), lambda i: (0, 0))],
        out_specs=pl.BlockSpec((tm, d), lambda i: (i, 0)),
        out_shape=jax.ShapeDtypeStruct((rows, d), out_dtype),
        compiler_params=_params(("parallel",)),
        name=name,
    )(x, g.reshape(1, d))


def _rmsnorm2_kernel(xp_ref, xs_ref, g_ref, o_ref, *, eps, n_p):
    i = pl.program_id(0)

    @pl.when(i < n_p)
    def _():
        o_ref[...] = _rms(xp_ref[...], g_ref[...], eps).astype(o_ref.dtype)

    @pl.when(i >= n_p)
    def _():
        o_ref[...] = _rms(xs_ref[...], g_ref[...], eps).astype(o_ref.dtype)


def rmsnorm_stacked(x_p, x_s, g, out_dtype, *, name, tm=512):
    (mp, d), ms = x_p.shape, x_s.shape[0]
    n_p = mp // tm
    return pl.pallas_call(
        functools.partial(_rmsnorm2_kernel, eps=EPS, n_p=n_p),
        grid=((mp + ms) // tm,),
        in_specs=[pl.BlockSpec((tm, d), lambda i: (jnp.minimum(i, n_p - 1), 0)),
                  pl.BlockSpec((tm, d), lambda i: (jnp.maximum(i - n_p, 0), 0)),
                  pl.BlockSpec((1, d), lambda i: (0, 0))],
        out_specs=pl.BlockSpec((tm, d), lambda i: (i, 0)),
        out_shape=jax.ShapeDtypeStruct((mp + ms, d), out_dtype),
        compiler_params=_params(("arbitrary",)),
        name=name,
    )(x_p, x_s, g.reshape(1, d))


def _mm_kernel(*refs, nk, n_extra, epilogue):
    a_ref, w_ref = refs[0], refs[1]
    extras = refs[2:2 + n_extra]
    o_ref = refs[2 + n_extra]
    def product():
        return jnp.dot(a_ref[...], w_ref[...].astype(bf16), preferred_element_type=f32)

    if nk == 1:
        o_ref[...] = epilogue(product(), *[e[...] for e in extras]).astype(o_ref.dtype)
    else:
        k = pl.program_id(2)

        @pl.when(k == 0)
        def _():
            o_ref[...] = extras[0][...] + product() if n_extra else product()

        @pl.when(k > 0)
        def _():
            o_ref[...] += product()


def matmul(a, w, *, name, n=None, col0=0, tm=TM, tn=TN, tk=None, out_dtype=f32, epilogue=None, extras=()):
    m, kdim = a.shape
    n = w.shape[1] if n is None else n
    tk = kdim if tk is None else tk
    nk = kdim // tk
    if nk > 1:
        assert epilogue is None and out_dtype == f32 and len(extras) <= 1
    if epilogue is None:
        epilogue = lambda acc: acc
    if col0 % tn == 0:
        w_spec = pl.BlockSpec((tk, tn), lambda i, j, k: (k, col0 // tn + j))
    else:
        w_spec = pl.BlockSpec((pl.Element(tk), pl.Element(tn)),
                              lambda i, j, k: (k * tk, pl.multiple_of(col0 + j * tn, LANES)))
    in_specs = [pl.BlockSpec((tm, tk), lambda i, j, k: (i, k)), w_spec]
    in_specs += [pl.BlockSpec(blk, imap) for (_, blk, imap) in extras]
    return pl.pallas_call(
        functools.partial(_mm_kernel, nk=nk, n_extra=len(extras), epilogue=epilogue),
        grid=(m // tm, n // tn, nk),
        in_specs=in_specs,
        out_specs=pl.BlockSpec((tm, tn), lambda i, j, k: (i, j)),
        out_shape=jax.ShapeDtypeStruct((m, n), out_dtype),
        compiler_params=_params(("parallel", "parallel", "arbitrary")),
        name=name,
    )(a, w, *[e[0] for e in extras])


def _tile(arr, tm=TM, tn=TN):
    return (arr, (tm, tn), lambda i, j, k: (i, j))


def _stacked_tiles(x_p, x_s, tm=TM, tn=TN):
    n_p = x_p.shape[0] // tm
    return [(x_p, (tm, tn), lambda i, j, k: (jnp.minimum(i, n_p - 1), j)),
            (x_s, (tm, tn), lambda i, j, k: (jnp.maximum(i - n_p, 0), j))]


def _pick_stacked(t_p, t_s, n_p):
    return jnp.where(pl.program_id(0) < n_p, t_p, t_s)


def _silu(x):
    return x * jax.nn.sigmoid(x)


def _softplus(x):
    return jnp.maximum(x, 0.0) + jnp.log1p(jnp.exp(-jnp.abs(x)))


def _expand_heads(x, rows):
    lane = lax.broadcasted_iota(jnp.int32, (rows, LANES), 1)
    tiles = []
    for k in range(HEADS_PER_GROUP // 2):
        a = jnp.broadcast_to(x[:, 2 * k:2 * k + 1], (rows, LANES))
        b = jnp.broadcast_to(x[:, 2 * k + 1:2 * k + 2], (rows, LANES))
        tiles.append(jnp.where(lane < HEAD_DIM, a, b))
    return jnp.concatenate(tiles, axis=1)


def _conv_silu(xpad, r0, cw_refs, cb_refs, q):
    w = jnp.concatenate([r[...] for r in cw_refs], axis=1)
    bias = jnp.concatenate([r[...] for r in cb_refs], axis=1)
    x = xpad[pl.ds(r0 - SUBLANES, q + SUBLANES), :]
    acc = w[0:1, :] * x
    for k in range(1, CONV_K):
        acc = pltpu.roll(acc, 1, axis=0) + w[k:k + 1, :] * x
    return _silu(acc[SUBLANES:, :] + bias)


def _stage(xpad, r0, nrows, xs, b, c):
    xpad[pl.ds(r0, nrows), 0:GROUP_W] = xs
    xpad[pl.ds(r0, nrows), GROUP_W:GROUP_W + D_STATE] = b
    xpad[pl.ds(r0, nrows), GROUP_W + D_STATE:XBC_W] = c


def _stage_halo(xpad, r0, pxs, pb, pc):
    xpad[pl.ds(r0 - SUBLANES, SUBLANES), :] = jnp.zeros((SUBLANES, XBC_W), f32)
    _stage(xpad, r0 - (CONV_K - 1), CONV_K - 1, pxs, pb, pc)


def _gate_norm(y, xs, dsk_ref, zs_ref, gout_ref):
    y = y + dsk_ref[...] * xs
    y = y * zs_ref[...].astype(f32)
    ms = jnp.mean(y * y, axis=-1, keepdims=True)
    return y * lax.rsqrt(ms + SSD_NORM_EPS) * gout_ref[...]


def _side_projection(h_ref, w_bf16, act):
    return act(jnp.dot(h_ref[...], w_bf16[...], preferred_element_type=f32))


def _decay_rows(cd_b):
    return jnp.concatenate(
        [jnp.broadcast_to(cd_b[h:h + 1, :], (HEAD_DIM, D_STATE)) for h in range(HEADS_PER_GROUP)], axis=0)


def _ssd_prompt_kernel(xs_ref, b_ref, c_ref, pxs_ref, pb_ref, pc_ref,
                       cwx_ref, cwb_ref, cwc_ref, cbx_ref, cbb_ref, cbc_ref,
                       dt_ref, dtt_ref, dtb_ref, dtbt_ref, alog_ref, alogt_ref,
                       dsk_ref, zs_ref, gout_ref, h_ref, wg_ref,
                       yn_ref, ssm_ref, sg_ref, xpad, state, wg_bf16, *, nc, n_mt, side_act):
    q = CHUNK
    r0 = SUBLANES
    s = pl.program_id(0)
    c = s % nc
    hi = lax.Precision.HIGHEST

    @pl.when(s % n_mt == 0)
    def _():
        wg_bf16[...] = wg_ref[...].astype(bf16)

    @pl.when(c == 0)
    def _():
        state[...] = jnp.zeros_like(state)
        _stage_halo(xpad, r0, pxs_ref[0], pb_ref[0], pc_ref[0])

    @pl.when(c > 0)
    def _():
        xpad[pl.ds(r0 - 3, 3), :] = xpad[pl.ds(r0 + q - 3, 3), :]

    sg_ref[...] = _side_projection(h_ref, wg_bf16, side_act).astype(sg_ref.dtype)
    _stage(xpad, r0, q, xs_ref[...], b_ref[...], c_ref[...])
    act = _conv_silu(xpad, r0, (cwx_ref, cwb_ref, cwc_ref), (cbx_ref, cbb_ref, cbc_ref), q)
    xs = act[:, 0:GROUP_W]
    bm = act[:, GROUP_W:GROUP_W + D_STATE].astype(bf16)
    cm = act[:, GROUP_W + D_STATE:XBC_W].astype(bf16)

    dt = _softplus(dt_ref[0] + dtb_ref[0])
    dta = dt * (-jnp.exp(alog_ref[0]))
    dtt = _softplus(dtt_ref[0] + dtbt_ref[0])
    dtat = dtt * (-jnp.exp(alogt_ref[0]))
    row = lax.broadcasted_iota(jnp.int32, (q, q), 0)
    col = lax.broadcasted_iota(jnp.int32, (q, q), 1)
    causal = row >= col
    acum = jnp.dot(causal.astype(f32), dta, precision=hi, preferred_element_type=f32)
    acum_t = jnp.dot(dtat, (row <= col).astype(f32), precision=hi, preferred_element_type=f32)
    tot_b = jnp.dot(dtat, jnp.ones((q, D_STATE), f32), precision=hi, preferred_element_type=f32)

    dt_x = _expand_heads(dt, q)
    acum_x = _expand_heads(acum, q)
    xdt = xs * dt_x
    decay_x = jnp.exp(acum_x[q - 1:q, :] - acum_x)
    xdtd = (xdt * decay_x).astype(bf16)
    xdt_b = xdt.astype(bf16)

    cb = lax.dot_general(cm, bm, (((1,), (1,)), ((), ())), preferred_element_type=f32)
    cb = jnp.where(causal, cb, 0.0)
    lane = lax.broadcasted_iota(jnp.int32, (q, LANES), 1)
    tiles = []
    for k in range(HEADS_PER_GROUP // 2):
        res = []
        for h in (2 * k, 2 * k + 1):
            seg = jnp.broadcast_to(acum[:, h:h + 1], (q, q)) - jnp.broadcast_to(acum_t[h:h + 1, :], (q, q))
            m_h = (cb * jnp.exp(jnp.minimum(seg, 0.0))).astype(bf16)
            res.append(jnp.dot(m_h, xdt_b[:, k * LANES:(k + 1) * LANES], preferred_element_type=f32))
        tiles.append(jnp.where(lane < HEAD_DIM, res[0], res[1]))
    y_diag = jnp.concatenate(tiles, axis=1)

    s_prev = state[...]
    y_off = lax.dot_general(cm, s_prev.astype(bf16), (((1,), (1,)), ((), ())), preferred_element_type=f32)
    y_off = y_off * jnp.exp(acum_x)
    new_states = lax.dot_general(xdtd, bm, (((0,), (0,)), ((), ())), preferred_element_type=f32)
    s_new = s_prev * _decay_rows(jnp.exp(tot_b)) + new_states
    state[...] = s_new
    yn_ref[...] = _gate_norm(y_diag + y_off, xs, dsk_ref, zs_ref, gout_ref).astype(yn_ref.dtype)

    @pl.when(c == nc - 1)
    def _():
        ssm_ref[0, 0] = state[...]


def _ssd_short_chunk(xpad, r0, cw_refs, cb_refs, dt_raw, dtb, alog, s_prev, q):
    hi = lax.Precision.HIGHEST
    act = _conv_silu(xpad, r0, cw_refs, cb_refs, q)
    xs = act[:, 0:GROUP_W]
    bm = act[:, GROUP_W:GROUP_W + D_STATE].astype(bf16)
    cm = act[:, GROUP_W + D_STATE:XBC_W].astype(bf16)

    dt = _softplus(dt_raw + dtb)
    dta = dt * (-jnp.exp(alog))
    row = lax.broadcasted_iota(jnp.int32, (q, q), 0)
    col = lax.broadcasted_iota(jnp.int32, (q, q), 1)
    acum = jnp.dot((row >= col).astype(f32), dta, precision=hi, preferred_element_type=f32)
    tot_b = lax.dot_general(dta, jnp.ones((q, D_STATE), f32), (((0,), (0,)), ((), ())),
                            precision=hi, preferred_element_type=f32)

    dt_x = _expand_heads(dt, q)
    acum_x = _expand_heads(acum, q)
    xdt = xs * dt_x
    decay_x = jnp.exp(acum_x[q - 1:q, :] - acum_x)
    xdtd = (xdt * decay_x).astype(bf16)

    cb = lax.dot_general(cm, bm, (((1,), (1,)), ((), ())), preferred_element_type=f32)
    rowx = lax.broadcasted_iota(jnp.int32, (q, GROUP_W), 0)
    y = jnp.zeros((q, GROUP_W), f32)
    for j in range(q):
        wj = jnp.where(rowx >= j, jnp.exp(acum_x - acum_x[j:j + 1, :]), 0.0)
        wj = wj * jnp.broadcast_to(cb[:, j:j + 1], (q, GROUP_W))
        y = y + wj * xdt[j:j + 1, :]

    y_off = lax.dot_general(cm, s_prev.astype(bf16), (((1,), (1,)), ((), ())), preferred_element_type=f32)
    y = y + y_off * jnp.exp(acum_x)
    new_states = lax.dot_general(xdtd, bm, (((0,), (0,)), ((), ())), preferred_element_type=f32)
    return y, xs, s_prev * _decay_rows(jnp.exp(tot_b)) + new_states


def _ssd_sample_kernel(xs_ref, b_ref, c_ref, pxs_ref, pb_ref, pc_ref,
                       cwx_ref, cwb_ref, cwc_ref, cbx_ref, cbb_ref, cbc_ref,
                       dt_ref, dtb_ref, alog_ref, dsk_ref, zs_ref, gout_ref, s0_ref,
                       yn_ref, ssm_ref, xpad, *, q, nseq):
    ys, xss, states = [], [], []
    stride = 2 * SUBLANES
    for n in range(nseq):
        r0 = n * stride + SUBLANES
        rows = pl.ds(n * q, q)
        _stage_halo(xpad, r0, pxs_ref[n], pb_ref[n], pc_ref[n])
        _stage(xpad, r0, q, xs_ref[rows, :], b_ref[rows, :], c_ref[rows, :])
        y, xs, s_new = _ssd_short_chunk(xpad, r0, (cwx_ref, cwb_ref, cwc_ref), (cbx_ref, cbb_ref, cbc_ref),
                                        dt_ref[0, rows, :], dtb_ref[0], alog_ref[0], s0_ref[n, 0], q)
        states.append(s_new)
        ys.append(y)
        xss.append(xs)
    yn = _gate_norm(jnp.concatenate(ys, axis=0), jnp.concatenate(xss, axis=0), dsk_ref, zs_ref, gout_ref)
    for n in range(nseq):
        ssm_ref[n, 0] = states[n]
    yn_ref[...] = yn.astype(yn_ref.dtype)


def _group_param(v):
    return v.reshape(N_GROUPS, 1, HEADS_PER_GROUP), v.reshape(N_GROUPS, HEADS_PER_GROUP, 1)


def ssd_branch(xbc, dt_raw, zs, conv_prev, s0, conv_w, conv_b, dt_bias, a_log, d_skip, g_out,
               *, row0, nb, seq, name, side_proj=None):
    m = xbc.shape[0]
    q = min(seq, CHUNK)
    nc = seq // q
    nseq = 1 if s0 is None else SAMPLE_SEQS_PER_STEP
    rows = nseq * q
    rb0 = row0 // rows
    dtg = dt_raw.reshape(m, N_GROUPS, HEADS_PER_GROUP).transpose(1, 0, 2)
    dtb, dtbt = _group_param(dt_bias)
    alog, alogt = _group_param(a_log)
    dsk = jnp.repeat(d_skip, HEAD_DIM).reshape(1, D_INNER)
    gout = g_out.reshape(1, D_INNER)
    cb2 = conv_b.reshape(1, CONV_DIM)
    nbg, ncg = D_INNER // D_STATE, (D_INNER + N_GROUPS * D_STATE) // D_STATE

    if s0 is None:
        steps = nb * N_GROUPS * nc
        ix = lambda f: (lambda s: f(s // (N_GROUPS * nc), (s // nc) % N_GROUPS,
                                    rb0 + (s // (N_GROUPS * nc)) * nc + s % nc))
    else:
        assert nc == 1 and nb % nseq == 0
        steps = (nb // nseq) * N_GROUPS
        ix = lambda f: (lambda s: f(s // N_GROUPS, s % N_GROUPS, rb0 + s // N_GROUPS))
    grid = (steps,)
    sem = ("arbitrary",)

    in_specs = [
        pl.BlockSpec((rows, GROUP_W), ix(lambda b, g, r: (r, g))),
        pl.BlockSpec((rows, D_STATE), ix(lambda b, g, r: (r, nbg + g))),
        pl.BlockSpec((rows, D_STATE), ix(lambda b, g, r: (r, ncg + g))),
        pl.BlockSpec((nseq, CONV_K - 1, GROUP_W), ix(lambda b, g, r: (b, 0, g))),
        pl.BlockSpec((nseq, CONV_K - 1, D_STATE), ix(lambda b, g, r: (b, 0, nbg + g))),
        pl.BlockSpec((nseq, CONV_K - 1, D_STATE), ix(lambda b, g, r: (b, 0, ncg + g))),
        pl.BlockSpec((CONV_K, GROUP_W), ix(lambda b, g, r: (0, g))),
        pl.BlockSpec((CONV_K, D_STATE), ix(lambda b, g, r: (0, nbg + g))),
        pl.BlockSpec((CONV_K, D_STATE), ix(lambda b, g, r: (0, ncg + g))),
        pl.BlockSpec((1, GROUP_W), ix(lambda b, g, r: (0, g))),
        pl.BlockSpec((1, D_STATE), ix(lambda b, g, r: (0, nbg + g))),
        pl.BlockSpec((1, D_STATE), ix(lambda b, g, r: (0, ncg + g))),
        pl.BlockSpec((1, rows, HEADS_PER_GROUP), ix(lambda b, g, r: (g, r, 0))),
    ]
    args = [xbc, xbc, xbc, conv_prev, conv_prev, conv_prev, conv_w, conv_w, conv_w, cb2, cb2, cb2, dtg]
    vec16 = pl.BlockSpec((1, 1, HEADS_PER_GROUP), ix(lambda b, g, r: (g, 0, 0)))
    vec16t = pl.BlockSpec((1, HEADS_PER_GROUP, 1), ix(lambda b, g, r: (g, 0, 0)))
    tail_specs = [
        pl.BlockSpec((1, GROUP_W), ix(lambda b, g, r: (0, g))),
        pl.BlockSpec((rows, GROUP_W), ix(lambda b, g, r: (r, g))),
        pl.BlockSpec((1, GROUP_W), ix(lambda b, g, r: (0, g))),
    ]
    state_spec = pl.BlockSpec((nseq, 1, GROUP_W, D_STATE), ix(lambda b, g, r: (b, g, 0, 0)))
    yn_spec = pl.BlockSpec((rows, GROUP_W), ix(lambda b, g, r: (r - rb0, g)))
    out_shape = (jax.ShapeDtypeStruct((nb * seq, D_INNER), bf16),
                 jax.ShapeDtypeStruct((nb, N_GROUPS, GROUP_W, D_STATE), f32))

    if s0 is None:
        dtgt = dt_raw.reshape(m, N_GROUPS, HEADS_PER_GROUP).transpose(1, 2, 0)
        in_specs += [pl.BlockSpec((1, HEADS_PER_GROUP, q), ix(lambda b, g, r: (g, 0, r))),
                     vec16, vec16t, vec16, vec16t] + tail_specs
        args += [dtgt, dtb, dtbt, alog, alogt, dsk, zs, gout]
        ph, pw, pcol0, pn, pact, pdtype = side_proj
        ptn = 2 * LANES
        n_nt = pn // ptn
        n_mt = steps // n_nt
        ptm = ph.shape[0] // n_mt
        assert n_mt * n_nt == steps and ptm * n_mt == ph.shape[0] and ptm % (2 * SUBLANES) == 0
        pk = ph.shape[1]
        in_specs += [pl.BlockSpec((ptm, pk), lambda s: (s % n_mt, 0)),
                     pl.BlockSpec((pl.Element(pk), pl.Element(ptn)),
                                  lambda s: (0, pl.multiple_of(pcol0 + (s // n_mt) * ptn, LANES)))]
        args += [ph, pw]
        out_specs = (yn_spec, state_spec, pl.BlockSpec((ptm, ptn), lambda s: (s % n_mt, s // n_mt)))
        out_shape = out_shape + (jax.ShapeDtypeStruct((ph.shape[0], pn), pdtype),)
        kern = functools.partial(_ssd_prompt_kernel, nc=nc, n_mt=n_mt, side_act=pact)
        scratch = [pltpu.VMEM((SUBLANES + q, XBC_W), f32), pltpu.VMEM((GROUP_W, D_STATE), f32),
                   pltpu.VMEM((pk, ptn), bf16)]
    else:
        in_specs += [vec16, vec16] + tail_specs + [state_spec]
        args += [dtb, alog, dsk, zs, gout, s0.reshape(nb, N_GROUPS, GROUP_W, D_STATE)]
        out_specs = (yn_spec, state_spec)
        kern = functools.partial(_ssd_sample_kernel, q=q, nseq=nseq)
        scratch = [pltpu.VMEM((nseq * 2 * SUBLANES, XBC_W), f32)]

    outs = pl.pallas_call(
        kern, grid=grid, in_specs=in_specs, out_specs=out_specs, out_shape=out_shape,
        scratch_shapes=scratch, compiler_params=_params(sem), name=name,
    )(*args)
    return (outs[0], outs[1].reshape(nb, N_HEADS, HEAD_DIM, D_STATE)) + tuple(outs[2:])


def _pool_kernel(u_ref, prev_ref, o_ref, halo, *, tl, n_past):
    lt = pl.program_id(1)
    base = 2 * SUBLANES

    @pl.when(lt == 0)
    def _():
        halo[pl.ds(0, base - POOL_KEEP), :] = jnp.zeros((base - POOL_KEEP, halo.shape[1]), f32)
        halo[pl.ds(base - POOL_KEEP, POOL_KEEP), :] = prev_ref[0]

    @pl.when(lt > 0)
    def _():
        halo[pl.ds(base - POOL_KEEP, POOL_KEEP), :] = halo[pl.ds(base + tl - POOL_KEEP, POOL_KEEP), :]

    halo[pl.ds(base, tl), :] = u_ref[...]
    t = lt * tl + lax.broadcasted_iota(jnp.int32, (tl, 1), 0) + (n_past + 1)
    for gi, w in enumerate(POOL_WINDOWS):
        cols = slice(gi * POOL_GROUP, (gi + 1) * POOL_GROUP)
        x = halo[:, cols]
        acc, shift = x, 1
        while shift < w:
            acc = acc + pltpu.roll(acc, shift, axis=0)
            shift *= 2
        cnt = jnp.minimum(t, w).astype(f32)
        o_ref[:, cols] = (acc[base:, :] / cnt - x[base:, :]).astype(o_ref.dtype)


def pool_branch(u, prev, *, row0, nb, seq, n_past, tl, name):
    d = u.shape[1]
    nlt = seq // tl
    rb0 = row0 // tl
    return pl.pallas_call(
        functools.partial(_pool_kernel, tl=tl, n_past=n_past),
        grid=(nb, nlt),
        in_specs=[pl.BlockSpec((tl, d), lambda b, l: (rb0 + b * nlt + l, 0)),
                  pl.BlockSpec((1, POOL_KEEP, d), lambda b, l: (b, 0, 0))],
        out_specs=pl.BlockSpec((tl, d), lambda b, l: (b * nlt + l, 0)),
        out_shape=jax.ShapeDtypeStruct((nb * seq, d), bf16),
        scratch_shapes=[pltpu.VMEM((2 * SUBLANES + tl, d), f32)],
        compiler_params=_params(("parallel", "arbitrary")),
        name=name,
    )(u, prev)


def _poolmix_kernel(ap_ref, as_ref, w_ref, s_ref, o_ref, *, n_p):
    def mix(a_ref):
        acc = jnp.dot(a_ref[...], w_ref[0].astype(bf16), preferred_element_type=f32)
        o_ref[...] = (acc * s_ref[...]).astype(o_ref.dtype)

    i = pl.program_id(0)
    pl.when(i < n_p)(lambda: mix(ap_ref))
    pl.when(i >= n_p)(lambda: mix(as_ref))


def pool_mix(pooled_p, pooled_s, w_mix, scale, *, tm, name):
    (mp, d), ms = pooled_p.shape, pooled_s.shape[0]
    n_p = mp // tm
    ng = d // POOL_GROUP
    return pl.pallas_call(
        functools.partial(_poolmix_kernel, n_p=n_p),
        grid=((mp + ms) // tm, ng),
        in_specs=[pl.BlockSpec((tm, POOL_GROUP), lambda i, g: (jnp.minimum(i, n_p - 1), g)),
                  pl.BlockSpec((tm, POOL_GROUP), lambda i, g: (jnp.maximum(i - n_p, 0), g)),
                  pl.BlockSpec((1, POOL_GROUP, POOL_GROUP), lambda i, g: (g, 0, 0)),
                  pl.BlockSpec((1, POOL_GROUP), lambda i, g: (0, g))],
        out_specs=pl.BlockSpec((tm, POOL_GROUP), lambda i, g: (i, g)),
        out_shape=jax.ShapeDtypeStruct((mp + ms, d), bf16),
        compiler_params=_params(("arbitrary", "arbitrary")),
        name=name,
    )(pooled_p, pooled_s, w_mix, scale.reshape(1, d))


def _ple_epilogue(acc, x2, p, wp):
    pe = jnp.dot(p, wp.astype(bf16), preferred_element_type=f32)
    return x2 + jax.nn.sigmoid(acc) * pe


def _last_rows(a2d, seq, keep, row0, nb):
    grp = 2 * SUBLANES
    a3 = a2d.reshape(a2d.shape[0] // grp, grp, a2d.shape[1])
    first = row0 // grp + seq // grp - 1
    last16 = lax.slice(a3, (first, 0, 0), (first + (nb - 1) * (seq // grp) + 1, grp, a2d.shape[1]),
                       (seq // grp, 1, 1))
    return last16[:, grp - keep:]


def kernel(x_prompt, x_sample, p_prompt, p_sample, state_ssm, state_conv, state_pool, g_mix, w_in, conv_w, conv_b, dt_bias, a_log, d_skip, g_ssd_out, w_ssd_out, w_pool_mix, pool_scale, w_pool_out, w_o, g_ffn, w_up, w_down, g_ple, w_ple_gate, w_ple_proj, g_final):
    bp, lp, d = x_prompt.shape
    bs, ls, _ = x_sample.shape
    mp, ms = bp * lp, bs * ls
    n_p = mp // TM
    xp2, xs2 = x_prompt.reshape(mp, d), x_sample.reshape(ms, d)
    pe = jnp.concatenate([p_prompt[0].reshape(mp, D_PLE), p_sample[0].reshape(ms, D_PLE)], axis=0).astype(bf16)

    c_xbc = D_INNER
    c_dt = c_xbc + CONV_DIM
    c_u = c_dt + N_HEADS
    c_gate = c_u + D_MODEL
    w_in0 = w_in[0]

    h = rmsnorm_stacked(xp2, xs2, g_mix[0], bf16, name="norm_mix")
    zs = matmul(h, w_in0, n=D_INNER, col0=0, tm=TM_BIG, out_dtype=bf16, epilogue=_silu, name="proj_z")
    xbc = matmul(h, w_in0, n=CONV_DIM, col0=c_xbc, tm=TM_BIG, name="proj_xbc")
    dt_raw = matmul(h, w_in0, n=N_HEADS, col0=c_dt, tm=TM_BIG, tn=N_HEADS, name="proj_dt")
    u = matmul(h, w_in0, n=D_MODEL, col0=c_u, tm=TM_BIG, name="proj_u")

    ssd_w = (conv_w[0], conv_b[0], dt_bias[0], a_log[0], d_skip[0], g_ssd_out[0])
    conv0_p = jnp.zeros((bp, CONV_K - 1, CONV_DIM), f32)
    yn_p, ssm_p, sg = ssd_branch(xbc, dt_raw, zs, conv0_p, None, *ssd_w, row0=0, nb=bp, seq=lp,
                                 name="ssd_prompt_gates",
                                 side_proj=(h, w_in0, c_gate, 2 * D_MODEL, jax.nn.sigmoid, bf16))
    yn_s, ssm_s = ssd_branch(xbc, dt_raw, zs, state_conv[0], state_ssm[0], *ssd_w, row0=mp, nb=bs, seq=ls,
                             name="ssd_sample")
    yn = jnp.concatenate([yn_p, yn_s], axis=0)

    pool0_p = jnp.zeros((bp, POOL_KEEP, d), f32)
    pooled_p = pool_branch(u, pool0_p, row0=0, nb=bp, seq=lp, n_past=0, tl=256, name="pool_prompt")
    pooled_s = pool_branch(u, state_pool[0], row0=mp, nb=bs, seq=ls, n_past=PAST_LEN, tl=ls, name="pool_sample")
    pm = pool_mix(pooled_p, pooled_s, w_pool_mix[0], pool_scale[0], tm=TM, name="pool_mix")

    ya = matmul(yn, w_ssd_out[0], **KT, name="ssd_out")
    merged = matmul(pm, w_pool_out[0], out_dtype=bf16, name="pool_out_merge",
                    epilogue=lambda yb, ga, gb, ya_t: ga.astype(f32) * ya_t + gb.astype(f32) * yb,
                    extras=[_tile(sg), (sg, (TM, TN), lambda i, j, k: (i, j + D_MODEL // TN)), _tile(ya)])
    x1 = matmul(merged, w_o[0], name="out_proj",
                epilogue=lambda acc, r_p, r_s: _pick_stacked(r_p, r_s, n_p) + acc,
                extras=_stacked_tiles(xp2, xs2))

    h2 = rmsnorm(x1, g_ffn[0], bf16, name="norm_ffn")
    act = matmul(h2, w_up[0], tm=TM_BIG, out_dtype=bf16, name="mlp_up",
                 epilogue=lambda acc: jnp.square(jnp.maximum(acc, 0.0)))
    x2 = matmul(act, w_down[0], **KT, name="mlp_down", extras=[_tile(x1, KT["tm"], KT["tn"])])

    h3 = rmsnorm(x2, g_ple[0], bf16, name="norm_ple")
    x3 = matmul(h3, w_ple_gate[0], name="ple", epilogue=_ple_epilogue,
                extras=[_tile(x2),
                        (pe, (TM, D_PLE), lambda i, j, k: (i, 0)),
                        (w_ple_proj[0], (D_PLE, TN), lambda i, j, k: (0, j))])

    y_p = rmsnorm(x3, g_final, f32, rows=mp, name="norm_final_prompt").reshape(bp, lp, d)
    y_s = rmsnorm(x3, g_final, f32, row_block0=mp // 512, rows=ms, name="norm_final_sample").reshape(bs, ls, d)

    conv_p = _last_rows(xbc, lp, CONV_K - 1, 0, bp)
    pool_p = _last_rows(u, lp, POOL_KEEP, 0, bp)
    xbc_s = xbc.reshape(-1, SUBLANES, CONV_DIM)[mp // SUBLANES:]
    u_s = u.reshape(-1, SUBLANES, d)[mp // SUBLANES:]
    conv_s = xbc_s[:, ls - (CONV_K - 1):]
    pool_s = jnp.concatenate([state_pool[0][:, ls:], u_s], axis=1)
    return (y_p, y_s, ssm_p[None], conv_p[None], pool_p[None], ssm_s[None], conv_s[None], pool_s[None])
```

```python
import functools

import jax
import jax.numpy as jnp
from jax import lax
from jax.experimental import pallas as pl
from jax.experimental.pallas import tpu as pltpu

f32 = jnp.float32
bf16 = jnp.bfloat16

D_MODEL = 4096
D_INNER = 8192
HEAD_DIM = 64
N_HEADS = 128
N_GROUPS = 8
HEADS_PER_GROUP = N_HEADS // N_GROUPS
GROUP_W = D_INNER // N_GROUPS
D_STATE = 128
CONV_K = 4
CONV_DIM = D_INNER + 2 * N_GROUPS * D_STATE
XBC_W = GROUP_W + 2 * D_STATE
POOL_WINDOWS = (2, 4, 8, 16)
POOL_GROUP = 1024
POOL_KEEP = 15
D_PLE = 256
EPS = 1e-6
SSD_NORM_EPS = 1e-5
CHUNK = 128
PAST_LEN = 16384

LANES = 128
SUBLANES = 8
VMEM_LIMIT = 56 * 1024 * 1024
TM = 1024
TM_BIG = 1536
TN = 512
KT = dict(tm=TM_BIG, tn=1024, tk=1024)
SAMPLE_SEQS_PER_STEP = 16


def _params(sem):
    return pltpu.CompilerParams(dimension_semantics=sem, vmem_limit_bytes=VMEM_LIMIT)


def _rms(x, g, eps):
    ms = jnp.mean(x * x, axis=-1, keepdims=True)
    return x * lax.rsqrt(ms + eps) * g


def _rmsnorm_kernel(x_ref, g_ref, o_ref, *, eps):
    o_ref[...] = _rms(x_ref[...], g_ref[...], eps).astype(o_ref.dtype)


def rmsnorm(x, g, out_dtype, *, name, tm=512, row_block0=0, rows=None):
    m, d = x.shape
    rows = m if rows is None else rows
    return pl.pallas_call(
        functools.partial(_rmsnorm_kernel, eps=EPS),
        grid=(rows // tm,),
        in_specs=[pl.BlockSpec((tm, d), lambda i: (i + row_block0, 0)),
                  pl.BlockSpec((1, d), lambda i: (0, 0))],
        out_specs=pl.BlockSpec((tm, d), lambda i: (i, 0)),
        out_shape=jax.ShapeDtypeStruct((rows, d), out_dtype),
        compiler_params=_params(("parallel",)),
        name=name,
    )(x, g.reshape(1, d))


def _rmsnorm2_kernel(xp_ref, xs_ref, g_ref, o_ref, *, eps, n_p):
    i = pl.program_id(0)

    @pl.when(i < n_p)
    def _():
        o_ref[...] = _rms(xp_ref[...], g_ref[...], eps).astype(o_ref.dtype)

    @pl.when(i >= n_p)
    def _():
        o_ref[...] = _rms(xs_ref[...], g_ref[...], eps).astype(o_ref.dtype)


def rmsnorm_stacked(x_p, x_s, g, out_dtype, *, name, tm=512):
    (mp, d), ms = x_p.shape, x_s.shape[0]
    n_p = mp // tm
    return pl.pallas_call(
        functools.partial(_rmsnorm2_kernel, eps=EPS, n_p=n_p),
        grid=((mp + ms) // tm,),
        in_specs=[pl.BlockSpec((tm, d), lambda i: (jnp.minimum(i, n_p - 1), 0)),
                  pl.BlockSpec((tm, d), lambda i: (jnp.maximum(i - n_p, 0), 0)),
                  pl.BlockSpec((1, d), lambda i: (0, 0))],
        out_specs=pl.BlockSpec((tm, d), lambda i: (i, 0)),
        out_shape=jax.ShapeDtypeStruct((mp + ms, d), out_dtype),
        compiler_params=_params(("arbitrary",)),
        name=name,
    )(x_p, x_s, g.reshape(1, d))


def _norm_parts(x, gain, hg_ref, ssq_ref):
    hg_ref[...] = (x * gain).astype(hg_ref.dtype)
    ssq_ref[0] = jnp.sum(x * x, axis=-1, keepdims=True)


def _mm_kernel(*refs, nk, n_extra, epilogue, with_norm):
    a_ref, w_ref = refs[0], refs[1]
    extras = refs[2:2 + n_extra]
    gain_ref = refs[2 + n_extra] if with_norm else None
    o_ref = refs[2 + n_extra + with_norm]
    norm_refs = refs[3 + n_extra + with_norm:]

    def product():
        return jnp.dot(a_ref[...], w_ref[...].astype(bf16), preferred_element_type=f32)

    if nk == 1:
        out = epilogue(product(), *[e[...] for e in extras])
        o_ref[...] = out.astype(o_ref.dtype)
        if with_norm:
            _norm_parts(out, gain_ref[...], *norm_refs)
    else:
        k = pl.program_id(2)

        @pl.when(k == 0)
        def _():
            o_ref[...] = extras[0][...] + product() if n_extra else product()

        @pl.when(k > 0)
        def _():
            o_ref[...] += product()

        if with_norm:
            @pl.when(k == nk - 1)
            def _():
                _norm_parts(o_ref[...], gain_ref[...], *norm_refs)


def matmul(a, w, *, name, n=None, col0=0, tm=TM, tn=TN, tk=None, out_dtype=f32, epilogue=None, extras=(),
           norm_gain=None):
    m, kdim = a.shape
    n = w.shape[1] if n is None else n
    tk = kdim if tk is None else tk
    nk = kdim // tk
    if nk > 1:
        assert epilogue is None and out_dtype == f32 and len(extras) <= 1
    if epilogue is None:
        epilogue = lambda acc: acc
    if col0 % tn == 0:
        w_spec = pl.BlockSpec((tk, tn), lambda i, j, k: (k, col0 // tn + j))
    else:
        w_spec = pl.BlockSpec((pl.Element(tk), pl.Element(tn)),
                              lambda i, j, k: (k * tk, pl.multiple_of(col0 + j * tn, LANES)))
    in_specs = [pl.BlockSpec((tm, tk), lambda i, j, k: (i, k)), w_spec]
    in_specs += [pl.BlockSpec(blk, imap) for (_, blk, imap) in extras]
    args = [a, w] + [e[0] for e in extras]
    out_specs = pl.BlockSpec((tm, tn), lambda i, j, k: (i, j))
    out_shape = jax.ShapeDtypeStruct((m, n), out_dtype)
    if norm_gain is not None:
        in_specs.append(pl.BlockSpec((1, tn), lambda i, j, k: (0, j)))
        args.append(norm_gain.reshape(1, n))
        out_specs = (out_specs, pl.BlockSpec((tm, tn), lambda i, j, k: (i, j)),
                     pl.BlockSpec((1, tm, 1), lambda i, j, k: (j, i, 0)))
        out_shape = (out_shape, jax.ShapeDtypeStruct((m, n), bf16), jax.ShapeDtypeStruct((n // tn, m, 1), f32))
    return pl.pallas_call(
        functools.partial(_mm_kernel, nk=nk, n_extra=len(extras), epilogue=epilogue,
                          with_norm=norm_gain is not None),
        grid=(m // tm, n // tn, nk),
        in_specs=in_specs,
        out_specs=out_specs,
        out_shape=out_shape,
        compiler_params=_params(("parallel", "parallel", "arbitrary")),
        name=name,
    )(*args)


def _row_scale(ssq, dim, eps):
    return lax.rsqrt(jnp.sum(ssq, axis=-1, keepdims=True) * (1.0 / dim) + eps)


def _ssq_tiles(ssq, tm):
    parts = ssq.shape[0]
    return (jnp.transpose(ssq[:, :, 0]), (tm, parts), lambda i, j, k: (i, 0))


def _tile(arr, tm=TM, tn=TN):
    return (arr, (tm, tn), lambda i, j, k: (i, j))


def _stacked_tiles(x_p, x_s, tm=TM, tn=TN):
    n_p = x_p.shape[0] // tm
    return [(x_p, (tm, tn), lambda i, j, k: (jnp.minimum(i, n_p - 1), j)),
            (x_s, (tm, tn), lambda i, j, k: (jnp.maximum(i - n_p, 0), j))]


def _pick_stacked(t_p, t_s, n_p):
    return jnp.where(pl.program_id(0) < n_p, t_p, t_s)


def _silu(x):
    return x * jax.nn.sigmoid(x)


def _softplus(x):
    return jnp.maximum(x, 0.0) + jnp.log1p(jnp.exp(-jnp.abs(x)))


def _expand_heads(x, rows):
    lane = lax.broadcasted_iota(jnp.int32, (rows, LANES), 1)
    tiles = []
    for k in range(HEADS_PER_GROUP // 2):
        a = jnp.broadcast_to(x[:, 2 * k:2 * k + 1], (rows, LANES))
        b = jnp.broadcast_to(x[:, 2 * k + 1:2 * k + 2], (rows, LANES))
        tiles.append(jnp.where(lane < HEAD_DIM, a, b))
    return jnp.concatenate(tiles, axis=1)


def _conv_silu(xpad, r0, cw_refs, cb_refs, q):
    w = jnp.concatenate([r[...] for r in cw_refs], axis=1)
    bias = jnp.concatenate([r[...] for r in cb_refs], axis=1)
    x = xpad[pl.ds(r0 - SUBLANES, q + SUBLANES), :]
    acc = w[0:1, :] * x
    for k in range(1, CONV_K):
        acc = pltpu.roll(acc, 1, axis=0) + w[k:k + 1, :] * x
    return _silu(acc[SUBLANES:, :] + bias)


def _stage(xpad, r0, nrows, xs, b, c):
    xpad[pl.ds(r0, nrows), 0:GROUP_W] = xs
    xpad[pl.ds(r0, nrows), GROUP_W:GROUP_W + D_STATE] = b
    xpad[pl.ds(r0, nrows), GROUP_W + D_STATE:XBC_W] = c


def _stage_halo(xpad, r0, pxs, pb, pc):
    xpad[pl.ds(r0 - SUBLANES, SUBLANES), :] = jnp.zeros((SUBLANES, XBC_W), f32)
    _stage(xpad, r0 - (CONV_K - 1), CONV_K - 1, pxs, pb, pc)


def _gate_norm(y, xs, dsk_ref, zs_ref, gout_ref):
    y = y + dsk_ref[...] * xs
    y = y * zs_ref[...].astype(f32)
    ms = jnp.mean(y * y, axis=-1, keepdims=True)
    return y * lax.rsqrt(ms + SSD_NORM_EPS) * gout_ref[...]


def _side_projection(h_ref, w_bf16, act):
    return act(jnp.dot(h_ref[...], w_bf16[...], preferred_element_type=f32))


def _decay_rows(cd_b):
    return jnp.concatenate(
        [jnp.broadcast_to(cd_b[h:h + 1, :], (HEAD_DIM, D_STATE)) for h in range(HEADS_PER_GROUP)], axis=0)


def _ssd_prompt_kernel(xs_ref, b_ref, c_ref, pxs_ref, pb_ref, pc_ref,
                       cwx_ref, cwb_ref, cwc_ref, cbx_ref, cbb_ref, cbc_ref,
                       dt_ref, dtt_ref, dtb_ref, dtbt_ref, alog_ref, alogt_ref,
                       dsk_ref, zs_ref, gout_ref, h_ref, wg_ref,
                       yn_ref, ssm_ref, sg_ref, xpad, state, wg_bf16, *, nc, n_mt, side_act):
    q = CHUNK
    r0 = SUBLANES
    s = pl.program_id(0)
    c = s % nc
    hi = lax.Precision.HIGHEST

    @pl.when(s % n_mt == 0)
    def _():
        wg_bf16[...] = wg_ref[...].astype(bf16)

    @pl.when(c == 0)
    def _():
        state[...] = jnp.zeros_like(state)
        _stage_halo(xpad, r0, pxs_ref[0], pb_ref[0], pc_ref[0])

    @pl.when(c > 0)
    def _():
        xpad[pl.ds(r0 - 3, 3), :] = xpad[pl.ds(r0 + q - 3, 3), :]

    sg_ref[...] = _side_projection(h_ref, wg_bf16, side_act).astype(sg_ref.dtype)
    _stage(xpad, r0, q, xs_ref[...], b_ref[...], c_ref[...])
    act = _conv_silu(xpad, r0, (cwx_ref, cwb_ref, cwc_ref), (cbx_ref, cbb_ref, cbc_ref), q)
    xs = act[:, 0:GROUP_W]
    bm = act[:, GROUP_W:GROUP_W + D_STATE].astype(bf16)
    cm = act[:, GROUP_W + D_STATE:XBC_W].astype(bf16)

    dt = _softplus(dt_ref[0] + dtb_ref[0])
    dta = dt * (-jnp.exp(alog_ref[0]))
    dtt = _softplus(dtt_ref[0] + dtbt_ref[0])
    dtat = dtt * (-jnp.exp(alogt_ref[0]))
    row = lax.broadcasted_iota(jnp.int32, (q, q), 0)
    col = lax.broadcasted_iota(jnp.int32, (q, q), 1)
    causal = row >= col
    acum = jnp.dot(causal.astype(f32), dta, precision=hi, preferred_element_type=f32)
    acum_t = jnp.dot(dtat, (row <= col).astype(f32), precision=hi, preferred_element_type=f32)
    tot_b = jnp.dot(dtat, jnp.ones((q, D_STATE), f32), precision=hi, preferred_element_type=f32)

    dt_x = _expand_heads(dt, q)
    acum_x = _expand_heads(acum, q)
    xdt = xs * dt_x
    decay_x = jnp.exp(acum_x[q - 1:q, :] - acum_x)
    xdtd = (xdt * decay_x).astype(bf16)
    xdt_b = xdt.astype(bf16)

    cb = lax.dot_general(cm, bm, (((1,), (1,)), ((), ())), preferred_element_type=f32)
    cb = jnp.where(causal, cb, 0.0)
    lane = lax.broadcasted_iota(jnp.int32, (q, LANES), 1)
    tiles = []
    for k in range(HEADS_PER_GROUP // 2):
        res = []
        for h in (2 * k, 2 * k + 1):
            seg = jnp.broadcast_to(acum[:, h:h + 1], (q, q)) - jnp.broadcast_to(acum_t[h:h + 1, :], (q, q))
            m_h = (cb * jnp.exp(jnp.minimum(seg, 0.0))).astype(bf16)
            res.append(jnp.dot(m_h, xdt_b[:, k * LANES:(k + 1) * LANES], preferred_element_type=f32))
        tiles.append(jnp.where(lane < HEAD_DIM, res[0], res[1]))
    y_diag = jnp.concatenate(tiles, axis=1)

    s_prev = state[...]
    y_off = lax.dot_general(cm, s_prev.astype(bf16), (((1,), (1,)), ((), ())), preferred_element_type=f32)
    y_off = y_off * jnp.exp(acum_x)
    new_states = lax.dot_general(xdtd, bm, (((0,), (0,)), ((), ())), preferred_element_type=f32)
    s_new = s_prev * _decay_rows(jnp.exp(tot_b)) + new_states
    state[...] = s_new
    yn_ref[...] = _gate_norm(y_diag + y_off, xs, dsk_ref, zs_ref, gout_ref).astype(yn_ref.dtype)

    @pl.when(c == nc - 1)
    def _():
        ssm_ref[0, 0] = state[...]


def _ssd_short_chunk(xpad, r0, cw_refs, cb_refs, dt_raw, dtb, alog, s_prev, q):
    hi = lax.Precision.HIGHEST
    act = _conv_silu(xpad, r0, cw_refs, cb_refs, q)
    xs = act[:, 0:GROUP_W]
    bm = act[:, GROUP_W:GROUP_W + D_STATE].astype(bf16)
    cm = act[:, GROUP_W + D_STATE:XBC_W].astype(bf16)

    dt = _softplus(dt_raw + dtb)
    dta = dt * (-jnp.exp(alog))
    row = lax.broadcasted_iota(jnp.int32, (q, q), 0)
    col = lax.broadcasted_iota(jnp.int32, (q, q), 1)
    acum = jnp.dot((row >= col).astype(f32), dta, precision=hi, preferred_element_type=f32)
    tot_b = lax.dot_general(dta, jnp.ones((q, D_STATE), f32), (((0,), (0,)), ((), ())),
                            precision=hi, preferred_element_type=f32)

    dt_x = _expand_heads(dt, q)
    acum_x = _expand_heads(acum, q)
    xdt = xs * dt_x
    decay_x = jnp.exp(acum_x[q - 1:q, :] - acum_x)
    xdtd = (xdt * decay_x).astype(bf16)

    cb = lax.dot_general(cm, bm, (((1,), (1,)), ((), ())), preferred_element_type=f32)
    rowx = lax.broadcasted_iota(jnp.int32, (q, GROUP_W), 0)
    y = jnp.zeros((q, GROUP_W), f32)
    for j in range(q):
        wj = jnp.where(rowx >= j, jnp.exp(acum_x - acum_x[j:j + 1, :]), 0.0)
        wj = wj * jnp.broadcast_to(cb[:, j:j + 1], (q, GROUP_W))
        y = y + wj * xdt[j:j + 1, :]

    y_off = lax.dot_general(cm, s_prev.astype(bf16), (((1,), (1,)), ((), ())), preferred_element_type=f32)
    y = y + y_off * jnp.exp(acum_x)
    new_states = lax.dot_general(xdtd, bm, (((0,), (0,)), ((), ())), preferred_element_type=f32)
    return y, xs, s_prev * _decay_rows(jnp.exp(tot_b)) + new_states


def _ssd_sample_kernel(xs_ref, b_ref, c_ref, pxs_ref, pb_ref, pc_ref,
                       cwx_ref, cwb_ref, cwc_ref, cbx_ref, cbb_ref, cbc_ref,
                       dt_ref, dtb_ref, alog_ref, dsk_ref, zs_ref, gout_ref, s0_ref,
                       yn_ref, ssm_ref, xpad, *, q, nseq):
    ys, xss, states = [], [], []
    stride = 2 * SUBLANES
    for n in range(nseq):
        r0 = n * stride + SUBLANES
        rows = pl.ds(n * q, q)
        _stage_halo(xpad, r0, pxs_ref[n], pb_ref[n], pc_ref[n])
        _stage(xpad, r0, q, xs_ref[rows, :], b_ref[rows, :], c_ref[rows, :])
        y, xs, s_new = _ssd_short_chunk(xpad, r0, (cwx_ref, cwb_ref, cwc_ref), (cbx_ref, cbb_ref, cbc_ref),
                                        dt_ref[0, rows, :], dtb_ref[0], alog_ref[0], s0_ref[n, 0], q)
        states.append(s_new)
        ys.append(y)
        xss.append(xs)
    yn = _gate_norm(jnp.concatenate(ys, axis=0), jnp.concatenate(xss, axis=0), dsk_ref, zs_ref, gout_ref)
    for n in range(nseq):
        ssm_ref[n, 0] = states[n]
    yn_ref[...] = yn.astype(yn_ref.dtype)


def _group_param(v):
    return v.reshape(N_GROUPS, 1, HEADS_PER_GROUP), v.reshape(N_GROUPS, HEADS_PER_GROUP, 1)


def ssd_branch(xbc, dt_raw, zs, conv_prev, s0, conv_w, conv_b, dt_bias, a_log, d_skip, g_out,
               *, row0, nb, seq, name, side_proj=None):
    m = xbc.shape[0]
    q = min(seq, CHUNK)
    nc = seq // q
    nseq = 1 if s0 is None else SAMPLE_SEQS_PER_STEP
    rows = nseq * q
    rb0 = row0 // rows
    dtg = dt_raw.reshape(m, N_GROUPS, HEADS_PER_GROUP).transpose(1, 0, 2)
    dtb, dtbt = _group_param(dt_bias)
    alog, alogt = _group_param(a_log)
    dsk = jnp.repeat(d_skip, HEAD_DIM).reshape(1, D_INNER)
    gout = g_out.reshape(1, D_INNER)
    cb2 = conv_b.reshape(1, CONV_DIM)
    nbg, ncg = D_INNER // D_STATE, (D_INNER + N_GROUPS * D_STATE) // D_STATE

    if s0 is None:
        steps = nb * N_GROUPS * nc
        ix = lambda f: (lambda s: f(s // (N_GROUPS * nc), (s // nc) % N_GROUPS,
                                    rb0 + (s // (N_GROUPS * nc)) * nc + s % nc))
    else:
        assert nc == 1 and nb % nseq == 0
        steps = (nb // nseq) * N_GROUPS
        ix = lambda f: (lambda s: f(s // N_GROUPS, s % N_GROUPS, rb0 + s // N_GROUPS))
    grid = (steps,)
    sem = ("arbitrary",)

    in_specs = [
        pl.BlockSpec((rows, GROUP_W), ix(lambda b, g, r: (r, g))),
        pl.BlockSpec((rows, D_STATE), ix(lambda b, g, r: (r, nbg + g))),
        pl.BlockSpec((rows, D_STATE), ix(lambda b, g, r: (r, ncg + g))),
        pl.BlockSpec((nseq, CONV_K - 1, GROUP_W), ix(lambda b, g, r: (b, 0, g))),
        pl.BlockSpec((nseq, CONV_K - 1, D_STATE), ix(lambda b, g, r: (b, 0, nbg + g))),
        pl.BlockSpec((nseq, CONV_K - 1, D_STATE), ix(lambda b, g, r: (b, 0, ncg + g))),
        pl.BlockSpec((CONV_K, GROUP_W), ix(lambda b, g, r: (0, g))),
        pl.BlockSpec((CONV_K, D_STATE), ix(lambda b, g, r: (0, nbg + g))),
        pl.BlockSpec((CONV_K, D_STATE), ix(lambda b, g, r: (0, ncg + g))),
        pl.BlockSpec((1, GROUP_W), ix(lambda b, g, r: (0, g))),
        pl.BlockSpec((1, D_STATE), ix(lambda b, g, r: (0, nbg + g))),
        pl.BlockSpec((1, D_STATE), ix(lambda b, g, r: (0, ncg + g))),
        pl.BlockSpec((1, rows, HEADS_PER_GROUP), ix(lambda b, g, r: (g, r, 0))),
    ]
    args = [xbc, xbc, xbc, conv_prev, conv_prev, conv_prev, conv_w, conv_w, conv_w, cb2, cb2, cb2, dtg]
    vec16 = pl.BlockSpec((1, 1, HEADS_PER_GROUP), ix(lambda b, g, r: (g, 0, 0)))
    vec16t = pl.BlockSpec((1, HEADS_PER_GROUP, 1), ix(lambda b, g, r: (g, 0, 0)))
    tail_specs = [
        pl.BlockSpec((1, GROUP_W), ix(lambda b, g, r: (0, g))),
        pl.BlockSpec((rows, GROUP_W), ix(lambda b, g, r: (r, g))),
        pl.BlockSpec((1, GROUP_W), ix(lambda b, g, r: (0, g))),
    ]
    state_spec = pl.BlockSpec((nseq, 1, GROUP_W, D_STATE), ix(lambda b, g, r: (b, g, 0, 0)))
    yn_spec = pl.BlockSpec((rows, GROUP_W), ix(lambda b, g, r: (r - rb0, g)))
    out_shape = (jax.ShapeDtypeStruct((nb * seq, D_INNER), bf16),
                 jax.ShapeDtypeStruct((nb, N_GROUPS, GROUP_W, D_STATE), f32))

    if s0 is None:
        dtgt = dt_raw.reshape(m, N_GROUPS, HEADS_PER_GROUP).transpose(1, 2, 0)
        in_specs += [pl.BlockSpec((1, HEADS_PER_GROUP, q), ix(lambda b, g, r: (g, 0, r))),
                     vec16, vec16t, vec16, vec16t] + tail_specs
        args += [dtgt, dtb, dtbt, alog, alogt, dsk, zs, gout]
        ph, pw, pcol0, pn, pact, pdtype = side_proj
        ptn = 2 * LANES
        n_nt = pn // ptn
        n_mt = steps // n_nt
        ptm = ph.shape[0] // n_mt
        assert n_mt * n_nt == steps and ptm * n_mt == ph.shape[0] and ptm % (2 * SUBLANES) == 0
        pk = ph.shape[1]
        in_specs += [pl.BlockSpec((ptm, pk), lambda s: (s % n_mt, 0)),
                     pl.BlockSpec((pl.Element(pk), pl.Element(ptn)),
                                  lambda s: (0, pl.multiple_of(pcol0 + (s // n_mt) * ptn, LANES)))]
        args += [ph, pw]
        out_specs = (yn_spec, state_spec, pl.BlockSpec((ptm, ptn), lambda s: (s % n_mt, s // n_mt)))
        out_shape = out_shape + (jax.ShapeDtypeStruct((ph.shape[0], pn), pdtype),)
        kern = functools.partial(_ssd_prompt_kernel, nc=nc, n_mt=n_mt, side_act=pact)
        scratch = [pltpu.VMEM((SUBLANES + q, XBC_W), f32), pltpu.VMEM((GROUP_W, D_STATE), f32),
                   pltpu.VMEM((pk, ptn), bf16)]
    else:
        in_specs += [vec16, vec16] + tail_specs + [state_spec]
        args += [dtb, alog, dsk, zs, gout, s0.reshape(nb, N_GROUPS, GROUP_W, D_STATE)]
        out_specs = (yn_spec, state_spec)
        kern = functools.partial(_ssd_sample_kernel, q=q, nseq=nseq)
        scratch = [pltpu.VMEM((nseq * 2 * SUBLANES, XBC_W), f32)]

    outs = pl.pallas_call(
        kern, grid=grid, in_specs=in_specs, out_specs=out_specs, out_shape=out_shape,
        scratch_shapes=scratch, compiler_params=_params(sem), name=name,
    )(*args)
    return (outs[0], outs[1].reshape(nb, N_HEADS, HEAD_DIM, D_STATE)) + tuple(outs[2:])


def _pool_kernel(u_ref, prev_ref, o_ref, halo, *, tl, n_past):
    lt = pl.program_id(1)
    base = 2 * SUBLANES

    @pl.when(lt == 0)
    def _():
        halo[pl.ds(0, base - POOL_KEEP), :] = jnp.zeros((base - POOL_KEEP, halo.shape[1]), f32)
        halo[pl.ds(base - POOL_KEEP, POOL_KEEP), :] = prev_ref[0]

    @pl.when(lt > 0)
    def _():
        halo[pl.ds(base - POOL_KEEP, POOL_KEEP), :] = halo[pl.ds(base + tl - POOL_KEEP, POOL_KEEP), :]

    halo[pl.ds(base, tl), :] = u_ref[...]
    t = lt * tl + lax.broadcasted_iota(jnp.int32, (tl, 1), 0) + (n_past + 1)
    for gi, w in enumerate(POOL_WINDOWS):
        cols = slice(gi * POOL_GROUP, (gi + 1) * POOL_GROUP)
        x = halo[:, cols]
        acc, shift = x, 1
        while shift < w:
            acc = acc + pltpu.roll(acc, shift, axis=0)
            shift *= 2
        cnt = jnp.minimum(t, w).astype(f32)
        o_ref[:, cols] = (acc[base:, :] / cnt - x[base:, :]).astype(o_ref.dtype)


def pool_branch(u, prev, *, row0, nb, seq, n_past, tl, name):
    d = u.shape[1]
    nlt = seq // tl
    rb0 = row0 // tl
    return pl.pallas_call(
        functools.partial(_pool_kernel, tl=tl, n_past=n_past),
        grid=(nb, nlt),
        in_specs=[pl.BlockSpec((tl, d), lambda b, l: (rb0 + b * nlt + l, 0)),
                  pl.BlockSpec((1, POOL_KEEP, d), lambda b, l: (b, 0, 0))],
        out_specs=pl.BlockSpec((tl, d), lambda b, l: (b * nlt + l, 0)),
        out_shape=jax.ShapeDtypeStruct((nb * seq, d), bf16),
        scratch_shapes=[pltpu.VMEM((2 * SUBLANES + tl, d), f32)],
        compiler_params=_params(("parallel", "arbitrary")),
        name=name,
    )(u, prev)


def _poolmix_kernel(ap_ref, as_ref, w_ref, s_ref, o_ref, *, n_p):
    def mix(a_ref):
        acc = jnp.dot(a_ref[...], w_ref[0].astype(bf16), preferred_element_type=f32)
        o_ref[...] = (acc * s_ref[...]).astype(o_ref.dtype)

    i = pl.program_id(0)
    pl.when(i < n_p)(lambda: mix(ap_ref))
    pl.when(i >= n_p)(lambda: mix(as_ref))


def pool_mix(pooled_p, pooled_s, w_mix, scale, *, tm, name):
    (mp, d), ms = pooled_p.shape, pooled_s.shape[0]
    n_p = mp // tm
    ng = d // POOL_GROUP
    return pl.pallas_call(
        functools.partial(_poolmix_kernel, n_p=n_p),
        grid=((mp + ms) // tm, ng),
        in_specs=[pl.BlockSpec((tm, POOL_GROUP), lambda i, g: (jnp.minimum(i, n_p - 1), g)),
                  pl.BlockSpec((tm, POOL_GROUP), lambda i, g: (jnp.maximum(i - n_p, 0), g)),
                  pl.BlockSpec((1, POOL_GROUP, POOL_GROUP), lambda i, g: (g, 0, 0)),
                  pl.BlockSpec((1, POOL_GROUP), lambda i, g: (0, g))],
        out_specs=pl.BlockSpec((tm, POOL_GROUP), lambda i, g: (i, g)),
        out_shape=jax.ShapeDtypeStruct((mp + ms, d), bf16),
        compiler_params=_params(("arbitrary", "arbitrary")),
        name=name,
    )(pooled_p, pooled_s, w_mix, scale.reshape(1, d))


def _ple_epilogue(acc, x2, p, wp, ssq):
    pe = jnp.dot(p, wp.astype(bf16), preferred_element_type=f32)
    return x2 + jax.nn.sigmoid(acc * _row_scale(ssq, D_MODEL, EPS)) * pe


def _last_rows(a2d, seq, keep, row0, nb):
    grp = 2 * SUBLANES
    a3 = a2d.reshape(a2d.shape[0] // grp, grp, a2d.shape[1])
    first = row0 // grp + seq // grp - 1
    last16 = lax.slice(a3, (first, 0, 0), (first + (nb - 1) * (seq // grp) + 1, grp, a2d.shape[1]),
                       (seq // grp, 1, 1))
    return last16[:, grp - keep:]


def kernel(x_prompt, x_sample, p_prompt, p_sample, state_ssm, state_conv, state_pool, g_mix, w_in, conv_w, conv_b, dt_bias, a_log, d_skip, g_ssd_out, w_ssd_out, w_pool_mix, pool_scale, w_pool_out, w_o, g_ffn, w_up, w_down, g_ple, w_ple_gate, w_ple_proj, g_final):
    bp, lp, d = x_prompt.shape
    bs, ls, _ = x_sample.shape
    mp, ms = bp * lp, bs * ls
    n_p = mp // TM
    xp2, xs2 = x_prompt.reshape(mp, d), x_sample.reshape(ms, d)
    pe = jnp.concatenate([p_prompt[0].reshape(mp, D_PLE), p_sample[0].reshape(ms, D_PLE)], axis=0).astype(bf16)

    c_xbc = D_INNER
    c_dt = c_xbc + CONV_DIM
    c_u = c_dt + N_HEADS
    c_gate = c_u + D_MODEL
    w_in0 = w_in[0]

    h = rmsnorm_stacked(xp2, xs2, g_mix[0], bf16, name="norm_mix")
    zs = matmul(h, w_in0, n=D_INNER, col0=0, tm=TM_BIG, out_dtype=bf16, epilogue=_silu, name="proj_z")
    xbc = matmul(h, w_in0, n=CONV_DIM, col0=c_xbc, tm=TM_BIG, name="proj_xbc")
    dt_raw = matmul(h, w_in0, n=N_HEADS, col0=c_dt, tm=TM_BIG, tn=N_HEADS, name="proj_dt")
    u = matmul(h, w_in0, n=D_MODEL, col0=c_u, tm=TM_BIG, name="proj_u")

    ssd_w = (conv_w[0], conv_b[0], dt_bias[0], a_log[0], d_skip[0], g_ssd_out[0])
    conv0_p = jnp.zeros((bp, CONV_K - 1, CONV_DIM), f32)
    yn_p, ssm_p, sg = ssd_branch(xbc, dt_raw, zs, conv0_p, None, *ssd_w, row0=0, nb=bp, seq=lp,
                                 name="ssd_prompt_gates",
                                 side_proj=(h, w_in0, c_gate, 2 * D_MODEL, jax.nn.sigmoid, bf16))
    yn_s, ssm_s = ssd_branch(xbc, dt_raw, zs, state_conv[0], state_ssm[0], *ssd_w, row0=mp, nb=bs, seq=ls,
                             name="ssd_sample")
    yn = jnp.concatenate([yn_p, yn_s], axis=0)

    pool0_p = jnp.zeros((bp, POOL_KEEP, d), f32)
    pooled_p = pool_branch(u, pool0_p, row0=0, nb=bp, seq=lp, n_past=0, tl=256, name="pool_prompt")
    pooled_s = pool_branch(u, state_pool[0], row0=mp, nb=bs, seq=ls, n_past=PAST_LEN, tl=ls, name="pool_sample")
    pm = pool_mix(pooled_p, pooled_s, w_pool_mix[0], pool_scale[0], tm=TM, name="pool_mix")

    ya = matmul(yn, w_ssd_out[0], **KT, name="ssd_out")
    merged = matmul(pm, w_pool_out[0], out_dtype=bf16, name="pool_out_merge",
                    epilogue=lambda yb, ga, gb, ya_t: ga.astype(f32) * ya_t + gb.astype(f32) * yb,
                    extras=[_tile(sg), (sg, (TM, TN), lambda i, j, k: (i, j + D_MODEL // TN)), _tile(ya)])
    x1, x1g, ssq1 = matmul(merged, w_o[0], name="out_proj", norm_gain=g_ffn[0],
                           epilogue=lambda acc, r_p, r_s: _pick_stacked(r_p, r_s, n_p) + acc,
                           extras=_stacked_tiles(xp2, xs2))

    act = matmul(x1g, w_up[0], tm=TM_BIG, out_dtype=bf16, name="mlp_up",
                 epilogue=lambda acc, ssq: jnp.square(jnp.maximum(acc * _row_scale(ssq, D_MODEL, EPS), 0.0)),
                 extras=[_ssq_tiles(ssq1, TM_BIG)])
    x2, x2g, ssq2 = matmul(act, w_down[0], **KT, name="mlp_down", norm_gain=g_ple[0],
                           extras=[_tile(x1, KT["tm"], KT["tn"])])

    x3 = matmul(x2g, w_ple_gate[0], name="ple", epilogue=_ple_epilogue,
                extras=[_tile(x2),
                        (pe, (TM, D_PLE), lambda i, j, k: (i, 0)),
                        (w_ple_proj[0], (D_PLE, TN), lambda i, j, k: (0, j)),
                        _ssq_tiles(ssq2, TM)])

    y_p = rmsnorm(x3, g_final, f32, rows=mp, name="norm_final_prompt").reshape(bp, lp, d)
    y_s = rmsnorm(x3, g_final, f32, row_block0=mp // 512, rows=ms, name="norm_final_sample").reshape(bs, ls, d)

    conv_p = _last_rows(xbc, lp, CONV_K - 1, 0, bp)
    pool_p = _last_rows(u, lp, POOL_KEEP, 0, bp)
    xbc_s = xbc.reshape(-1, SUBLANES, CONV_DIM)[mp // SUBLANES:]
    u_s = u.reshape(-1, SUBLANES, d)[mp // SUBLANES:]
    conv_s = xbc_s[:, ls - (CONV_K - 1):]
    pool_s = jnp.concatenate([state_pool[0][:, ls:], u_s], axis=1)
    return (y_p, y_s, ssm_p[None], conv_p[None], pool_p[None], ssm_s[None], conv_s[None], pool_s[None])
```

```python
import functools

import jax
import jax.numpy as jnp
from jax import lax
from jax.experimental import pallas as pl
from jax.experimental.pallas import tpu as pltpu

f32 = jnp.float32
bf16 = jnp.bfloat16

D_MODEL = 4096
D_INNER = 8192
HEAD_DIM = 64
N_HEADS = 128
N_GROUPS = 8
HEADS_PER_GROUP = N_HEADS // N_GROUPS
GROUP_W = D_INNER // N_GROUPS
D_STATE = 128
CONV_K = 4
CONV_DIM = D_INNER + 2 * N_GROUPS * D_STATE
XBC_W = GROUP_W + 2 * D_STATE
POOL_WINDOWS = (2, 4, 8, 16)
POOL_GROUP = 1024
POOL_KEEP = 15
D_PLE = 256
EPS = 1e-6
SSD_NORM_EPS = 1e-5
CHUNK = 128
PAST_LEN = 16384

LANES = 128
SUBLANES = 8
VMEM_LIMIT = 56 * 1024 * 1024
TM = 1024
TM_BIG = 1536
TN = 512
KT = dict(tm=TM_BIG, tn=1024, tk=1024)
SAMPLE_SEQS_PER_STEP = 16


def _params(sem):
    return pltpu.CompilerParams(dimension_semantics=sem, vmem_limit_bytes=VMEM_LIMIT)


def _rms(x, g, eps):
    ms = jnp.mean(x * x, axis=-1, keepdims=True)
    return x * lax.rsqrt(ms + eps) * g


def _rmsnorm_kernel(x_ref, g_ref, o_ref, *, eps):
    o_ref[...] = _rms(x_ref[...], g_ref[...], eps).astype(o_ref.dtype)


def rmsnorm(x, g, out_dtype, *, name, tm=512, row_block0=0, rows=None):
    m, d = x.shape
    rows = m if rows is None else rows
    return pl.pallas_call(
        functools.partial(_rmsnorm_kernel, eps=EPS),
        grid=(rows // tm,),
        in_specs=[pl.BlockSpec((tm, d), lambda i: (i + row_block0, 0)),
                  pl.BlockSpec((1, d), lambda i: (0, 0))],
        out_specs=pl.BlockSpec((tm, d), lambda i: (i, 0)),
        out_shape=jax.ShapeDtypeStruct((rows, d), out_dtype),
        compiler_params=_params(("parallel",)),
        name=name,
    )(x, g.reshape(1, d))


def _rmsnorm2_kernel(xp_ref, xs_ref, g_ref, o_ref, *, eps, n_p):
    i = pl.program_id(0)

    @pl.when(i < n_p)
    def _():
        o_ref[...] = _rms(xp_ref[...], g_ref[...], eps).astype(o_ref.dtype)

    @pl.when(i >= n_p)
    def _():
        o_ref[...] = _rms(xs_ref[...], g_ref[...], eps).astype(o_ref.dtype)


def rmsnorm_stacked(x_p, x_s, g, out_dtype, *, name, tm=512):
    (mp, d), ms = x_p.shape, x_s.shape[0]
    n_p = mp // tm
    return pl.pallas_call(
        functools.partial(_rmsnorm2_kernel, eps=EPS, n_p=n_p),
        grid=((mp + ms) // tm,),
        in_specs=[pl.BlockSpec((tm, d), lambda i: (jnp.minimum(i, n_p - 1), 0)),
                  pl.BlockSpec((tm, d), lambda i: (jnp.maximum(i - n_p, 0), 0)),
                  pl.BlockSpec((1, d), lambda i: (0, 0))],
        out_specs=pl.BlockSpec((tm, d), lambda i: (i, 0)),
        out_shape=jax.ShapeDtypeStruct((mp + ms, d), out_dtype),
        compiler_params=_params(("arbitrary",)),
        name=name,
    )(x_p, x_s, g.reshape(1, d))


def _norm_parts(x, gain, hg_ref, ssq_ref):
    hg_ref[...] = (x * gain).astype(hg_ref.dtype)
    part = jnp.broadcast_to(jnp.sum(x * x, axis=-1, keepdims=True), ssq_ref.shape)
    j = pl.program_id(1)

    @pl.when(j == 0)
    def _():
        ssq_ref[...] = part

    @pl.when(j > 0)
    def _():
        ssq_ref[...] += part


def _mm_kernel(*refs, nk, n_extra, epilogue, with_norm):
    a_ref, w_ref = refs[0], refs[1]
    extras = refs[2:2 + n_extra]
    gain_ref = refs[2 + n_extra] if with_norm else None
    o_ref = refs[2 + n_extra + with_norm]
    norm_refs = refs[3 + n_extra + with_norm:]

    def product():
        return jnp.dot(a_ref[...], w_ref[...].astype(bf16), preferred_element_type=f32)

    if nk == 1:
        out = epilogue(product(), *[e[...] for e in extras])
        o_ref[...] = out.astype(o_ref.dtype)
        if with_norm:
            _norm_parts(out, gain_ref[...], *norm_refs)
    else:
        k = pl.program_id(2)

        @pl.when(k == 0)
        def _():
            o_ref[...] = extras[0][...] + product() if n_extra else product()

        @pl.when(k > 0)
        def _():
            o_ref[...] += product()

        if with_norm:
            @pl.when(k == nk - 1)
            def _():
                _norm_parts(o_ref[...], gain_ref[...], *norm_refs)


def matmul(a, w, *, name, n=None, col0=0, tm=TM, tn=TN, tk=None, out_dtype=f32, epilogue=None, extras=(),
           norm_gain=None):
    m, kdim = a.shape
    n = w.shape[1] if n is None else n
    tk = kdim if tk is None else tk
    nk = kdim // tk
    if nk > 1:
        assert epilogue is None and out_dtype == f32 and len(extras) <= 1
    if epilogue is None:
        epilogue = lambda acc: acc
    if col0 % tn == 0:
        w_spec = pl.BlockSpec((tk, tn), lambda i, j, k: (k, col0 // tn + j))
    else:
        w_spec = pl.BlockSpec((pl.Element(tk), pl.Element(tn)),
                              lambda i, j, k: (k * tk, pl.multiple_of(col0 + j * tn, LANES)))
    in_specs = [pl.BlockSpec((tm, tk), lambda i, j, k: (i, k)), w_spec]
    in_specs += [pl.BlockSpec(blk, imap) for (_, blk, imap) in extras]
    args = [a, w] + [e[0] for e in extras]
    out_specs = pl.BlockSpec((tm, tn), lambda i, j, k: (i, j))
    out_shape = jax.ShapeDtypeStruct((m, n), out_dtype)
    if norm_gain is not None:
        in_specs.append(pl.BlockSpec((1, tn), lambda i, j, k: (0, j)))
        args.append(norm_gain.reshape(1, n))
        out_specs = (out_specs, pl.BlockSpec((tm, tn), lambda i, j, k: (i, j)),
                     pl.BlockSpec((tm, LANES), lambda i, j, k: (i, 0)))
        out_shape = (out_shape, jax.ShapeDtypeStruct((m, n), bf16), jax.ShapeDtypeStruct((m, LANES), f32))
    col_sem = "parallel" if norm_gain is None else "arbitrary"
    return pl.pallas_call(
        functools.partial(_mm_kernel, nk=nk, n_extra=len(extras), epilogue=epilogue,
                          with_norm=norm_gain is not None),
        grid=(m // tm, n // tn, nk),
        in_specs=in_specs,
        out_specs=out_specs,
        out_shape=out_shape,
        compiler_params=_params(("parallel", col_sem, "arbitrary")),
        name=name,
    )(*args)


def _row_scale(ssq, dim, eps):
    return lax.rsqrt(ssq[:, 0:1] * (1.0 / dim) + eps)


def _ssq_tiles(ssq, tm):
    return (ssq, (tm, LANES), lambda i, j, k: (i, 0))


def _tile(arr, tm=TM, tn=TN):
    return (arr, (tm, tn), lambda i, j, k: (i, j))


def _stacked_tiles(x_p, x_s, tm=TM, tn=TN):
    n_p = x_p.shape[0] // tm
    return [(x_p, (tm, tn), lambda i, j, k: (jnp.minimum(i, n_p - 1), j)),
            (x_s, (tm, tn), lambda i, j, k: (jnp.maximum(i - n_p, 0), j))]


def _pick_stacked(t_p, t_s, n_p):
    return jnp.where(pl.program_id(0) < n_p, t_p, t_s)


def _silu(x):
    return x * jax.nn.sigmoid(x)


def _softplus(x):
    return jnp.maximum(x, 0.0) + jnp.log1p(jnp.exp(-jnp.abs(x)))


def _expand_heads(x, rows):
    lane = lax.broadcasted_iota(jnp.int32, (rows, LANES), 1)
    tiles = []
    for k in range(HEADS_PER_GROUP // 2):
        a = jnp.broadcast_to(x[:, 2 * k:2 * k + 1], (rows, LANES))
        b = jnp.broadcast_to(x[:, 2 * k + 1:2 * k + 2], (rows, LANES))
        tiles.append(jnp.where(lane < HEAD_DIM, a, b))
    return jnp.concatenate(tiles, axis=1)


def _conv_silu(xpad, r0, cw_refs, cb_refs, q):
    w = jnp.concatenate([r[...] for r in cw_refs], axis=1)
    bias = jnp.concatenate([r[...] for r in cb_refs], axis=1)
    x = xpad[pl.ds(r0 - SUBLANES, q + SUBLANES), :]
    acc = w[0:1, :] * x
    for k in range(1, CONV_K):
        acc = pltpu.roll(acc, 1, axis=0) + w[k:k + 1, :] * x
    return _silu(acc[SUBLANES:, :] + bias)


def _stage(xpad, r0, nrows, xs, b, c):
    xpad[pl.ds(r0, nrows), 0:GROUP_W] = xs
    xpad[pl.ds(r0, nrows), GROUP_W:GROUP_W + D_STATE] = b
    xpad[pl.ds(r0, nrows), GROUP_W + D_STATE:XBC_W] = c


def _stage_halo(xpad, r0, pxs, pb, pc):
    xpad[pl.ds(r0 - SUBLANES, SUBLANES), :] = jnp.zeros((SUBLANES, XBC_W), f32)
    _stage(xpad, r0 - (CONV_K - 1), CONV_K - 1, pxs, pb, pc)


def _gate_norm(y, xs, dsk_ref, zs_ref, gout_ref):
    y = y + dsk_ref[...] * xs
    y = y * zs_ref[...].astype(f32)
    ms = jnp.mean(y * y, axis=-1, keepdims=True)
    return y * lax.rsqrt(ms + SSD_NORM_EPS) * gout_ref[...]


def _side_projection(h_ref, w_bf16, act):
    return act(jnp.dot(h_ref[...], w_bf16[...], preferred_element_type=f32))


def _decay_rows(cd_b):
    return jnp.concatenate(
        [jnp.broadcast_to(cd_b[h:h + 1, :], (HEAD_DIM, D_STATE)) for h in range(HEADS_PER_GROUP)], axis=0)


def _ssd_prompt_kernel(xs_ref, b_ref, c_ref, pxs_ref, pb_ref, pc_ref,
                       cwx_ref, cwb_ref, cwc_ref, cbx_ref, cbb_ref, cbc_ref,
                       dt_ref, dtt_ref, dtb_ref, dtbt_ref, alog_ref, alogt_ref,
                       dsk_ref, zs_ref, gout_ref, h_ref, wg_ref,
                       yn_ref, ssm_ref, sg_ref, xpad, state, wg_bf16, *, nc, n_mt, side_act):
    q = CHUNK
    r0 = SUBLANES
    s = pl.program_id(0)
    c = s % nc
    hi = lax.Precision.HIGHEST

    @pl.when(s % n_mt == 0)
    def _():
        wg_bf16[...] = wg_ref[...].astype(bf16)

    @pl.when(c == 0)
    def _():
        state[...] = jnp.zeros_like(state)
        _stage_halo(xpad, r0, pxs_ref[0], pb_ref[0], pc_ref[0])

    @pl.when(c > 0)
    def _():
        xpad[pl.ds(r0 - 3, 3), :] = xpad[pl.ds(r0 + q - 3, 3), :]

    sg_ref[...] = _side_projection(h_ref, wg_bf16, side_act).astype(sg_ref.dtype)
    _stage(xpad, r0, q, xs_ref[...], b_ref[...], c_ref[...])
    act = _conv_silu(xpad, r0, (cwx_ref, cwb_ref, cwc_ref), (cbx_ref, cbb_ref, cbc_ref), q)
    xs = act[:, 0:GROUP_W]
    bm = act[:, GROUP_W:GROUP_W + D_STATE].astype(bf16)
    cm = act[:, GROUP_W + D_STATE:XBC_W].astype(bf16)

    dt = _softplus(dt_ref[0] + dtb_ref[0])
    dta = dt * (-jnp.exp(alog_ref[0]))
    dtt = _softplus(dtt_ref[0] + dtbt_ref[0])
    dtat = dtt * (-jnp.exp(alogt_ref[0]))
    row = lax.broadcasted_iota(jnp.int32, (q, q), 0)
    col = lax.broadcasted_iota(jnp.int32, (q, q), 1)
    causal = row >= col
    acum = jnp.dot(causal.astype(f32), dta, precision=hi, preferred_element_type=f32)
    acum_t = jnp.dot(dtat, (row <= col).astype(f32), precision=hi, preferred_element_type=f32)
    tot_b = jnp.dot(dtat, jnp.ones((q, D_STATE), f32), precision=hi, preferred_element_type=f32)

    dt_x = _expand_heads(dt, q)
    acum_x = _expand_heads(acum, q)
    xdt = xs * dt_x
    decay_x = jnp.exp(acum_x[q - 1:q, :] - acum_x)
    xdtd = (xdt * decay_x).astype(bf16)
    xdt_b = xdt.astype(bf16)

    cb = lax.dot_general(cm, bm, (((1,), (1,)), ((), ())), preferred_element_type=f32)
    cb = jnp.where(causal, cb, 0.0)
    lane = lax.broadcasted_iota(jnp.int32, (q, LANES), 1)
    tiles = []
    for k in range(HEADS_PER_GROUP // 2):
        res = []
        for h in (2 * k, 2 * k + 1):
            seg = jnp.broadcast_to(acum[:, h:h + 1], (q, q)) - jnp.broadcast_to(acum_t[h:h + 1, :], (q, q))
            m_h = (cb * jnp.exp(jnp.minimum(seg, 0.0))).astype(bf16)
            res.append(jnp.dot(m_h, xdt_b[:, k * LANES:(k + 1) * LANES], preferred_element_type=f32))
        tiles.append(jnp.where(lane < HEAD_DIM, res[0], res[1]))
    y_diag = jnp.concatenate(tiles, axis=1)

    s_prev = state[...]
    y_off = lax.dot_general(cm, s_prev.astype(bf16), (((1,), (1,)), ((), ())), preferred_element_type=f32)
    y_off = y_off * jnp.exp(acum_x)
    new_states = lax.dot_general(xdtd, bm, (((0,), (0,)), ((), ())), preferred_element_type=f32)
    s_new = s_prev * _decay_rows(jnp.exp(tot_b)) + new_states
    state[...] = s_new
    yn_ref[...] = _gate_norm(y_diag + y_off, xs, dsk_ref, zs_ref, gout_ref).astype(yn_ref.dtype)

    @pl.when(c == nc - 1)
    def _():
        ssm_ref[0, 0] = state[...]


def _ssd_short_chunk(xpad, r0, cw_refs, cb_refs, dt_raw, dtb, alog, s_prev, q):
    hi = lax.Precision.HIGHEST
    act = _conv_silu(xpad, r0, cw_refs, cb_refs, q)
    xs = act[:, 0:GROUP_W]
    bm = act[:, GROUP_W:GROUP_W + D_STATE].astype(bf16)
    cm = act[:, GROUP_W + D_STATE:XBC_W].astype(bf16)

    dt = _softplus(dt_raw + dtb)
    dta = dt * (-jnp.exp(alog))
    row = lax.broadcasted_iota(jnp.int32, (q, q), 0)
    col = lax.broadcasted_iota(jnp.int32, (q, q), 1)
    acum = jnp.dot((row >= col).astype(f32), dta, precision=hi, preferred_element_type=f32)
    tot_b = lax.dot_general(dta, jnp.ones((q, D_STATE), f32), (((0,), (0,)), ((), ())),
                            precision=hi, preferred_element_type=f32)

    dt_x = _expand_heads(dt, q)
    acum_x = _expand_heads(acum, q)
    xdt = xs * dt_x
    decay_x = jnp.exp(acum_x[q - 1:q, :] - acum_x)
    xdtd = (xdt * decay_x).astype(bf16)

    cb = lax.dot_general(cm, bm, (((1,), (1,)), ((), ())), preferred_element_type=f32)
    rowx = lax.broadcasted_iota(jnp.int32, (q, GROUP_W), 0)
    y = jnp.zeros((q, GROUP_W), f32)
    for j in range(q):
        wj = jnp.where(rowx >= j, jnp.exp(acum_x - acum_x[j:j + 1, :]), 0.0)
        wj = wj * jnp.broadcast_to(cb[:, j:j + 1], (q, GROUP_W))
        y = y + wj * xdt[j:j + 1, :]

    y_off = lax.dot_general(cm, s_prev.astype(bf16), (((1,), (1,)), ((), ())), preferred_element_type=f32)
    y = y + y_off * jnp.exp(acum_x)
    new_states = lax.dot_general(xdtd, bm, (((0,), (0,)), ((), ())), preferred_element_type=f32)
    return y, xs, s_prev * _decay_rows(jnp.exp(tot_b)) + new_states


def _ssd_sample_kernel(xs_ref, b_ref, c_ref, pxs_ref, pb_ref, pc_ref,
                       cwx_ref, cwb_ref, cwc_ref, cbx_ref, cbb_ref, cbc_ref,
                       dt_ref, dtb_ref, alog_ref, dsk_ref, zs_ref, gout_ref, s0_ref,
                       yn_ref, ssm_ref, xpad, *, q, nseq):
    ys, xss, states = [], [], []
    stride = 2 * SUBLANES
    for n in range(nseq):
        r0 = n * stride + SUBLANES
        rows = pl.ds(n * q, q)
        _stage_halo(xpad, r0, pxs_ref[n], pb_ref[n], pc_ref[n])
        _stage(xpad, r0, q, xs_ref[rows, :], b_ref[rows, :], c_ref[rows, :])
        y, xs, s_new = _ssd_short_chunk(xpad, r0, (cwx_ref, cwb_ref, cwc_ref), (cbx_ref, cbb_ref, cbc_ref),
                                        dt_ref[0, rows, :], dtb_ref[0], alog_ref[0], s0_ref[n, 0], q)
        states.append(s_new)
        ys.append(y)
        xss.append(xs)
    yn = _gate_norm(jnp.concatenate(ys, axis=0), jnp.concatenate(xss, axis=0), dsk_ref, zs_ref, gout_ref)
    for n in range(nseq):
        ssm_ref[n, 0] = states[n]
    yn_ref[...] = yn.astype(yn_ref.dtype)


def _group_param(v):
    return v.reshape(N_GROUPS, 1, HEADS_PER_GROUP), v.reshape(N_GROUPS, HEADS_PER_GROUP, 1)


def ssd_branch(xbc, dt_raw, zs, conv_prev, s0, conv_w, conv_b, dt_bias, a_log, d_skip, g_out,
               *, row0, nb, seq, name, side_proj=None):
    m = xbc.shape[0]
    q = min(seq, CHUNK)
    nc = seq // q
    nseq = 1 if s0 is None else SAMPLE_SEQS_PER_STEP
    rows = nseq * q
    rb0 = row0 // rows
    dtg = dt_raw.reshape(m, N_GROUPS, HEADS_PER_GROUP).transpose(1, 0, 2)
    dtb, dtbt = _group_param(dt_bias)
    alog, alogt = _group_param(a_log)
    dsk = jnp.repeat(d_skip, HEAD_DIM).reshape(1, D_INNER)
    gout = g_out.reshape(1, D_INNER)
    cb2 = conv_b.reshape(1, CONV_DIM)
    nbg, ncg = D_INNER // D_STATE, (D_INNER + N_GROUPS * D_STATE) // D_STATE

    if s0 is None:
        steps = nb * N_GROUPS * nc
        ix = lambda f: (lambda s: f(s // (N_GROUPS * nc), (s // nc) % N_GROUPS,
                                    rb0 + (s // (N_GROUPS * nc)) * nc + s % nc))
    else:
        assert nc == 1 and nb % nseq == 0
        steps = (nb // nseq) * N_GROUPS
        ix = lambda f: (lambda s: f(s // N_GROUPS, s % N_GROUPS, rb0 + s // N_GROUPS))
    grid = (steps,)
    sem = ("arbitrary",)

    in_specs = [
        pl.BlockSpec((rows, GROUP_W), ix(lambda b, g, r: (r, g))),
        pl.BlockSpec((rows, D_STATE), ix(lambda b, g, r: (r, nbg + g))),
        pl.BlockSpec((rows, D_STATE), ix(lambda b, g, r: (r, ncg + g))),
        pl.BlockSpec((nseq, CONV_K - 1, GROUP_W), ix(lambda b, g, r: (b, 0, g))),
        pl.BlockSpec((nseq, CONV_K - 1, D_STATE), ix(lambda b, g, r: (b, 0, nbg + g))),
        pl.BlockSpec((nseq, CONV_K - 1, D_STATE), ix(lambda b, g, r: (b, 0, ncg + g))),
        pl.BlockSpec((CONV_K, GROUP_W), ix(lambda b, g, r: (0, g))),
        pl.BlockSpec((CONV_K, D_STATE), ix(lambda b, g, r: (0, nbg + g))),
        pl.BlockSpec((CONV_K, D_STATE), ix(lambda b, g, r: (0, ncg + g))),
        pl.BlockSpec((1, GROUP_W), ix(lambda b, g, r: (0, g))),
        pl.BlockSpec((1, D_STATE), ix(lambda b, g, r: (0, nbg + g))),
        pl.BlockSpec((1, D_STATE), ix(lambda b, g, r: (0, ncg + g))),
        pl.BlockSpec((1, rows, HEADS_PER_GROUP), ix(lambda b, g, r: (g, r, 0))),
    ]
    args = [xbc, xbc, xbc, conv_prev, conv_prev, conv_prev, conv_w, conv_w, conv_w, cb2, cb2, cb2, dtg]
    vec16 = pl.BlockSpec((1, 1, HEADS_PER_GROUP), ix(lambda b, g, r: (g, 0, 0)))
    vec16t = pl.BlockSpec((1, HEADS_PER_GROUP, 1), ix(lambda b, g, r: (g, 0, 0)))
    tail_specs = [
        pl.BlockSpec((1, GROUP_W), ix(lambda b, g, r: (0, g))),
        pl.BlockSpec((rows, GROUP_W), ix(lambda b, g, r: (r, g))),
        pl.BlockSpec((1, GROUP_W), ix(lambda b, g, r: (0, g))),
    ]
    state_spec = pl.BlockSpec((nseq, 1, GROUP_W, D_STATE), ix(lambda b, g, r: (b, g, 0, 0)))
    yn_spec = pl.BlockSpec((rows, GROUP_W), ix(lambda b, g, r: (r - rb0, g)))
    out_shape = (jax.ShapeDtypeStruct((nb * seq, D_INNER), bf16),
                 jax.ShapeDtypeStruct((nb, N_GROUPS, GROUP_W, D_STATE), f32))

    if s0 is None:
        dtgt = dt_raw.reshape(m, N_GROUPS, HEADS_PER_GROUP).transpose(1, 2, 0)
        in_specs += [pl.BlockSpec((1, HEADS_PER_GROUP, q), ix(lambda b, g, r: (g, 0, r))),
                     vec16, vec16t, vec16, vec16t] + tail_specs
        args += [dtgt, dtb, dtbt, alog, alogt, dsk, zs, gout]
        ph, pw, pcol0, pn, pact, pdtype = side_proj
        ptn = 2 * LANES
        n_nt = pn // ptn
        n_mt = steps // n_nt
        ptm = ph.shape[0] // n_mt
        assert n_mt * n_nt == steps and ptm * n_mt == ph.shape[0] and ptm % (2 * SUBLANES) == 0
        pk = ph.shape[1]
        in_specs += [pl.BlockSpec((ptm, pk), lambda s: (s % n_mt, 0)),
                     pl.BlockSpec((pl.Element(pk), pl.Element(ptn)),
                                  lambda s: (0, pl.multiple_of(pcol0 + (s // n_mt) * ptn, LANES)))]
        args += [ph, pw]
        out_specs = (yn_spec, state_spec, pl.BlockSpec((ptm, ptn), lambda s: (s % n_mt, s // n_mt)))
        out_shape = out_shape + (jax.ShapeDtypeStruct((ph.shape[0], pn), pdtype),)
        kern = functools.partial(_ssd_prompt_kernel, nc=nc, n_mt=n_mt, side_act=pact)
        scratch = [pltpu.VMEM((SUBLANES + q, XBC_W), f32), pltpu.VMEM((GROUP_W, D_STATE), f32),
                   pltpu.VMEM((pk, ptn), bf16)]
    else:
        in_specs += [vec16, vec16] + tail_specs + [state_spec]
        args += [dtb, alog, dsk, zs, gout, s0.reshape(nb, N_GROUPS, GROUP_W, D_STATE)]
        out_specs = (yn_spec, state_spec)
        kern = functools.partial(_ssd_sample_kernel, q=q, nseq=nseq)
        scratch = [pltpu.VMEM((nseq * 2 * SUBLANES, XBC_W), f32)]

    outs = pl.pallas_call(
        kern, grid=grid, in_specs=in_specs, out_specs=out_specs, out_shape=out_shape,
        scratch_shapes=scratch, compiler_params=_params(sem), name=name,
    )(*args)
    return (outs[0], outs[1].reshape(nb, N_HEADS, HEAD_DIM, D_STATE)) + tuple(outs[2:])


def _pool_kernel(u_ref, prev_ref, o_ref, halo, *, tl, n_past, nseq, carry):
    lt = pl.program_id(1)
    base = 2 * SUBLANES
    stride = base + tl

    def init_halo(n):
        halo[pl.ds(n * stride, base - POOL_KEEP), :] = jnp.zeros((base - POOL_KEEP, halo.shape[1]), f32)
        halo[pl.ds(n * stride + base - POOL_KEEP, POOL_KEEP), :] = prev_ref[n]

    if carry:
        assert nseq == 1
        pl.when(lt == 0)(lambda: init_halo(0))

        @pl.when(lt > 0)
        def _():
            halo[pl.ds(base - POOL_KEEP, POOL_KEEP), :] = halo[pl.ds(base + tl - POOL_KEEP, POOL_KEEP), :]
    else:
        for n in range(nseq):
            init_halo(n)

    for n in range(nseq):
        halo[pl.ds(n * stride + base, tl), :] = u_ref[pl.ds(n * tl, tl), :]
    t = lt * tl + lax.broadcasted_iota(jnp.int32, (tl, 1), 0) + (n_past + 1)
    for gi, w in enumerate(POOL_WINDOWS):
        cols = slice(gi * POOL_GROUP, (gi + 1) * POOL_GROUP)
        cnt = jnp.minimum(t, w).astype(f32)
        outs = []
        for n in range(nseq):
            x = halo[pl.ds(n * stride, stride), cols]
            acc, shift = x, 1
            while shift < w:
                acc = acc + pltpu.roll(acc, shift, axis=0)
                shift *= 2
            outs.append(acc[base:, :] / cnt - x[base:, :])
        o_ref[:, cols] = jnp.concatenate(outs, axis=0).astype(o_ref.dtype)


def pool_branch(u, prev, *, row0, nb, seq, n_past, tl, name, nseq=1):
    d = u.shape[1]
    nlt = seq // tl
    assert nseq == 1 or nlt == 1
    rows = nseq * tl
    rb0 = row0 // rows
    return pl.pallas_call(
        functools.partial(_pool_kernel, tl=tl, n_past=n_past, nseq=nseq, carry=nlt > 1),
        grid=(nb // nseq, nlt),
        in_specs=[pl.BlockSpec((rows, d), lambda b, l: (rb0 + b * nlt + l, 0)),
                  pl.BlockSpec((nseq, POOL_KEEP, d), lambda b, l: (b, 0, 0))],
        out_specs=pl.BlockSpec((rows, d), lambda b, l: (b * nlt + l, 0)),
        out_shape=jax.ShapeDtypeStruct((nb * seq, d), bf16),
        scratch_shapes=[pltpu.VMEM((nseq * (2 * SUBLANES + tl), d), f32)],
        compiler_params=_params(("parallel", "arbitrary")),
        name=name,
    )(u, prev)


def _poolmix_kernel(ap_ref, as_ref, w_ref, s_ref, o_ref, *, n_p):
    def mix(a_ref):
        acc = jnp.dot(a_ref[...], w_ref[0].astype(bf16), preferred_element_type=f32)
        o_ref[...] = (acc * s_ref[...]).astype(o_ref.dtype)

    i = pl.program_id(0)
    pl.when(i < n_p)(lambda: mix(ap_ref))
    pl.when(i >= n_p)(lambda: mix(as_ref))


def pool_mix(pooled_p, pooled_s, w_mix, scale, *, tm, name):
    (mp, d), ms = pooled_p.shape, pooled_s.shape[0]
    n_p = mp // tm
    ng = d // POOL_GROUP
    return pl.pallas_call(
        functools.partial(_poolmix_kernel, n_p=n_p),
        grid=((mp + ms) // tm, ng),
        in_specs=[pl.BlockSpec((tm, POOL_GROUP), lambda i, g: (jnp.minimum(i, n_p - 1), g)),
                  pl.BlockSpec((tm, POOL_GROUP), lambda i, g: (jnp.maximum(i - n_p, 0), g)),
                  pl.BlockSpec((1, POOL_GROUP, POOL_GROUP), lambda i, g: (g, 0, 0)),
                  pl.BlockSpec((1, POOL_GROUP), lambda i, g: (0, g))],
        out_specs=pl.BlockSpec((tm, POOL_GROUP), lambda i, g: (i, g)),
        out_shape=jax.ShapeDtypeStruct((mp + ms, d), bf16),
        compiler_params=_params(("arbitrary", "arbitrary")),
        name=name,
    )(pooled_p, pooled_s, w_mix, scale.reshape(1, d))


def _ple_epilogue(acc, x2, p, wp, ssq):
    pe = jnp.dot(p, wp.astype(bf16), preferred_element_type=f32)
    return x2 + jax.nn.sigmoid(acc * _row_scale(ssq, D_MODEL, EPS)) * pe


def _last_rows(a2d, seq, keep, row0, nb):
    grp = 2 * SUBLANES
    a3 = a2d.reshape(a2d.shape[0] // grp, grp, a2d.shape[1])
    first = row0 // grp + seq // grp - 1
    last16 = lax.slice(a3, (first, 0, 0), (first + (nb - 1) * (seq // grp) + 1, grp, a2d.shape[1]),
                       (seq // grp, 1, 1))
    return last16[:, grp - keep:]


def kernel(x_prompt, x_sample, p_prompt, p_sample, state_ssm, state_conv, state_pool, g_mix, w_in, conv_w, conv_b, dt_bias, a_log, d_skip, g_ssd_out, w_ssd_out, w_pool_mix, pool_scale, w_pool_out, w_o, g_ffn, w_up, w_down, g_ple, w_ple_gate, w_ple_proj, g_final):
    bp, lp, d = x_prompt.shape
    bs, ls, _ = x_sample.shape
    mp, ms = bp * lp, bs * ls
    n_p = mp // TM
    xp2, xs2 = x_prompt.reshape(mp, d), x_sample.reshape(ms, d)
    pe = jnp.concatenate([p_prompt[0].reshape(mp, D_PLE), p_sample[0].reshape(ms, D_PLE)], axis=0).astype(bf16)

    c_xbc = D_INNER
    c_dt = c_xbc + CONV_DIM
    c_u = c_dt + N_HEADS
    c_gate = c_u + D_MODEL
    w_in0 = w_in[0]

    h = rmsnorm_stacked(xp2, xs2, g_mix[0], bf16, name="norm_mix")
    zs = matmul(h, w_in0, n=D_INNER, col0=0, tm=TM_BIG, out_dtype=bf16, epilogue=_silu, name="proj_z")
    xbc = matmul(h, w_in0, n=CONV_DIM, col0=c_xbc, tm=TM_BIG, name="proj_xbc")
    dt_raw = matmul(h, w_in0, n=N_HEADS, col0=c_dt, tm=TM_BIG, tn=N_HEADS, name="proj_dt")
    u = matmul(h, w_in0, n=D_MODEL, col0=c_u, tm=TM_BIG, name="proj_u")

    ssd_w = (conv_w[0], conv_b[0], dt_bias[0], a_log[0], d_skip[0], g_ssd_out[0])
    conv0_p = jnp.zeros((bp, CONV_K - 1, CONV_DIM), f32)
    yn_p, ssm_p, sg = ssd_branch(xbc, dt_raw, zs, conv0_p, None, *ssd_w, row0=0, nb=bp, seq=lp,
                                 name="ssd_prompt_gates",
                                 side_proj=(h, w_in0, c_gate, 2 * D_MODEL, jax.nn.sigmoid, bf16))
    yn_s, ssm_s = ssd_branch(xbc, dt_raw, zs, state_conv[0], state_ssm[0], *ssd_w, row0=mp, nb=bs, seq=ls,
                             name="ssd_sample")
    yn = jnp.concatenate([yn_p, yn_s], axis=0)

    pool0_p = jnp.zeros((bp, POOL_KEEP, d), f32)
    pooled_p = pool_branch(u, pool0_p, row0=0, nb=bp, seq=lp, n_past=0, tl=256, name="pool_prompt")
    pooled_s = pool_branch(u, state_pool[0], row0=mp, nb=bs, seq=ls, n_past=PAST_LEN, tl=ls,
                           nseq=SAMPLE_SEQS_PER_STEP, name="pool_sample")
    pm = pool_mix(pooled_p, pooled_s, w_pool_mix[0], pool_scale[0], tm=TM, name="pool_mix")

    ya = matmul(yn, w_ssd_out[0], **KT, name="ssd_out")
    merged = matmul(pm, w_pool_out[0], out_dtype=bf16, name="pool_out_merge",
                    epilogue=lambda yb, ga, gb, ya_t: ga.astype(f32) * ya_t + gb.astype(f32) * yb,
                    extras=[_tile(sg), (sg, (TM, TN), lambda i, j, k: (i, j + D_MODEL // TN)), _tile(ya)])
    x1, x1g, ssq1 = matmul(merged, w_o[0], name="out_proj", norm_gain=g_ffn[0],
                           epilogue=lambda acc, r_p, r_s: _pick_stacked(r_p, r_s, n_p) + acc,
                           extras=_stacked_tiles(xp2, xs2))

    act = matmul(x1g, w_up[0], tm=TM_BIG, out_dtype=bf16, name="mlp_up",
                 epilogue=lambda acc, ssq: jnp.square(jnp.maximum(acc * _row_scale(ssq, D_MODEL, EPS), 0.0)),
                 extras=[_ssq_tiles(ssq1, TM_BIG)])
    x2, x2g, ssq2 = matmul(act, w_down[0], **KT, name="mlp_down", norm_gain=g_ple[0],
                           extras=[_tile(x1, KT["tm"], KT["tn"])])

    x3 = matmul(x2g, w_ple_gate[0], name="ple", epilogue=_ple_epilogue,
                extras=[_tile(x2),
                        (pe, (TM, D_PLE), lambda i, j, k: (i, 0)),
                        (w_ple_proj[0], (D_PLE, TN), lambda i, j, k: (0, j)),
                        _ssq_tiles(ssq2, TM)])

    y_p = rmsnorm(x3, g_final, f32, rows=mp, name="norm_final_prompt").reshape(bp, lp, d)
    y_s = rmsnorm(x3, g_final, f32, row_block0=mp // 512, rows=ms, name="norm_final_sample").reshape(bs, ls, d)

    conv_p = _last_rows(xbc, lp, CONV_K - 1, 0, bp)
    pool_p = _last_rows(u, lp, POOL_KEEP, 0, bp)
    xbc_s = xbc.reshape(-1, SUBLANES, CONV_DIM)[mp // SUBLANES:]
    u_s = u.reshape(-1, SUBLANES, d)[mp // SUBLANES:]
    conv_s = xbc_s[:, ls - (CONV_K - 1):]
    pool_s = jnp.concatenate([state_pool[0][:, ls:], u_s], axis=1)
    return (y_p, y_s, ssm_p[None], conv_p[None], pool_p[None], ssm_s[None], conv_s[None], pool_s[None])
```

```python
import functools

import jax
import jax.numpy as jnp
from jax import lax
from jax.experimental import pallas as pl
from jax.experimental.pallas import tpu as pltpu

f32 = jnp.float32
bf16 = jnp.bfloat16

D_MODEL = 4096
D_INNER = 8192
HEAD_DIM = 64
N_HEADS = 128
N_GROUPS = 8
HEADS_PER_GROUP = N_HEADS // N_GROUPS
GROUP_W = D_INNER // N_GROUPS
D_STATE = 128
CONV_K = 4
CONV_DIM = D_INNER + 2 * N_GROUPS * D_STATE
XBC_W = GROUP_W + 2 * D_STATE
POOL_WINDOWS = (2, 4, 8, 16)
POOL_GROUP = 1024
POOL_KEEP = 15
D_PLE = 256
EPS = 1e-6
SSD_NORM_EPS = 1e-5
CHUNK = 128
PAST_LEN = 16384

LANES = 128
SUBLANES = 8
VMEM_LIMIT = 56 * 1024 * 1024
TM = 1024
TM_BIG = 1536
TN = 512
KT = dict(tm=TM_BIG, tn=1024, tk=1024)
SAMPLE_SEQS_PER_STEP = 16


def _params(sem):
    return pltpu.CompilerParams(dimension_semantics=sem, vmem_limit_bytes=VMEM_LIMIT)


def _rms(x, g, eps):
    ms = jnp.mean(x * x, axis=-1, keepdims=True)
    return x * lax.rsqrt(ms + eps) * g


def _rmsnorm_kernel(x_ref, g_ref, o_ref, *, eps):
    o_ref[...] = _rms(x_ref[...], g_ref[...], eps).astype(o_ref.dtype)


def rmsnorm(x, g, out_dtype, *, name, tm=512, row_block0=0, rows=None):
    m, d = x.shape
    rows = m if rows is None else rows
    return pl.pallas_call(
        functools.partial(_rmsnorm_kernel, eps=EPS),
        grid=(rows // tm,),
        in_specs=[pl.BlockSpec((tm, d), lambda i: (i + row_block0, 0)),
                  pl.BlockSpec((1, d), lambda i: (0, 0))],
        out_specs=pl.BlockSpec((tm, d), lambda i: (i, 0)),
        out_shape=jax.ShapeDtypeStruct((rows, d), out_dtype),
        compiler_params=_params(("parallel",)),
        name=name,
    )(x, g.reshape(1, d))


def _rmsnorm2_kernel(xp_ref, xs_ref, g_ref, o_ref, *, eps, n_p):
    i = pl.program_id(0)

    @pl.when(i < n_p)
    def _():
        o_ref[...] = _rms(xp_ref[...], g_ref[...], eps).astype(o_ref.dtype)

    @pl.when(i >= n_p)
    def _():
        o_ref[...] = _rms(xs_ref[...], g_ref[...], eps).astype(o_ref.dtype)


def rmsnorm_stacked(x_p, x_s, g, out_dtype, *, name, tm=512):
    (mp, d), ms = x_p.shape, x_s.shape[0]
    n_p = mp // tm
    return pl.pallas_call(
        functools.partial(_rmsnorm2_kernel, eps=EPS, n_p=n_p),
        grid=((mp + ms) // tm,),
        in_specs=[pl.BlockSpec((tm, d), lambda i: (jnp.minimum(i, n_p - 1), 0)),
                  pl.BlockSpec((tm, d), lambda i: (jnp.maximum(i - n_p, 0), 0)),
                  pl.BlockSpec((1, d), lambda i: (0, 0))],
        out_specs=pl.BlockSpec((tm, d), lambda i: (i, 0)),
        out_shape=jax.ShapeDtypeStruct((mp + ms, d), out_dtype),
        compiler_params=_params(("arbitrary",)),
        name=name,
    )(x_p, x_s, g.reshape(1, d))


def _norm_parts(x, gain, hg_ref, ssq_ref):
    hg_ref[...] = (x * gain).astype(hg_ref.dtype)
    part = jnp.broadcast_to(jnp.sum(x * x, axis=-1, keepdims=True), ssq_ref.shape)
    j = pl.program_id(1)

    @pl.when(j == 0)
    def _():
        ssq_ref[...] = part

    @pl.when(j > 0)
    def _():
        ssq_ref[...] += part


def _mm_kernel(*refs, nk, n_extra, epilogue, with_norm):
    a_ref, w_ref = refs[0], refs[1]
    extras = refs[2:2 + n_extra]
    gain_ref = refs[2 + n_extra] if with_norm else None
    o_ref = refs[2 + n_extra + with_norm]
    norm_refs = refs[3 + n_extra + with_norm:]

    def product():
        return jnp.dot(a_ref[...], w_ref[...].astype(bf16), preferred_element_type=f32)

    if nk == 1:
        out = epilogue(product(), *[e[...] for e in extras])
        o_ref[...] = out.astype(o_ref.dtype)
        if with_norm:
            _norm_parts(out, gain_ref[...], *norm_refs)
    else:
        k = pl.program_id(2)

        @pl.when(k == 0)
        def _():
            o_ref[...] = extras[0][...] + product() if n_extra else product()

        @pl.when(k > 0)
        def _():
            o_ref[...] += product()

        if with_norm:
            @pl.when(k == nk - 1)
            def _():
                _norm_parts(o_ref[...], gain_ref[...], *norm_refs)


def matmul(a, w, *, name, n=None, col0=0, tm=TM, tn=TN, tk=None, out_dtype=f32, epilogue=None, extras=(),
           norm_gain=None):
    m, kdim = a.shape
    n = w.shape[1] if n is None else n
    tk = kdim if tk is None else tk
    nk = kdim // tk
    if nk > 1:
        assert epilogue is None and out_dtype == f32 and len(extras) <= 1
    if epilogue is None:
        epilogue = lambda acc: acc
    if col0 % tn == 0:
        w_spec = pl.BlockSpec((tk, tn), lambda i, j, k: (k, col0 // tn + j))
    else:
        w_spec = pl.BlockSpec((pl.Element(tk), pl.Element(tn)),
                              lambda i, j, k: (k * tk, pl.multiple_of(col0 + j * tn, LANES)))
    in_specs = [pl.BlockSpec((tm, tk), lambda i, j, k: (i, k)), w_spec]
    in_specs += [pl.BlockSpec(blk, imap) for (_, blk, imap) in extras]
    args = [a, w] + [e[0] for e in extras]
    out_specs = pl.BlockSpec((tm, tn), lambda i, j, k: (i, j))
    out_shape = jax.ShapeDtypeStruct((m, n), out_dtype)
    if norm_gain is not None:
        in_specs.append(pl.BlockSpec((1, tn), lambda i, j, k: (0, j)))
        args.append(norm_gain.reshape(1, n))
        out_specs = (out_specs, pl.BlockSpec((tm, tn), lambda i, j, k: (i, j)),
                     pl.BlockSpec((tm, LANES), lambda i, j, k: (i, 0)))
        out_shape = (out_shape, jax.ShapeDtypeStruct((m, n), bf16), jax.ShapeDtypeStruct((m, LANES), f32))
    col_sem = "parallel" if norm_gain is None else "arbitrary"
    return pl.pallas_call(
        functools.partial(_mm_kernel, nk=nk, n_extra=len(extras), epilogue=epilogue,
                          with_norm=norm_gain is not None),
        grid=(m // tm, n // tn, nk),
        in_specs=in_specs,
        out_specs=out_specs,
        out_shape=out_shape,
        compiler_params=_params(("parallel", col_sem, "arbitrary")),
        name=name,
    )(*args)


def _row_scale(ssq, dim, eps):
    return lax.rsqrt(ssq[:, 0:1] * (1.0 / dim) + eps)


def _ssq_tiles(ssq, tm):
    return (ssq, (tm, LANES), lambda i, j, k: (i, 0))


def _tile(arr, tm=TM, tn=TN):
    return (arr, (tm, tn), lambda i, j, k: (i, j))


def _stacked_tiles(x_p, x_s, tm=TM, tn=TN):
    n_p = x_p.shape[0] // tm
    return [(x_p, (tm, tn), lambda i, j, k: (jnp.minimum(i, n_p - 1), j)),
            (x_s, (tm, tn), lambda i, j, k: (jnp.maximum(i - n_p, 0), j))]


def _pick_stacked(t_p, t_s, n_p):
    return jnp.where(pl.program_id(0) < n_p, t_p, t_s)


def _silu(x):
    return x * jax.nn.sigmoid(x)


def _softplus(x):
    return jnp.maximum(x, 0.0) + jnp.log1p(jnp.exp(-jnp.abs(x)))


def _expand_heads(x, rows):
    lane = lax.broadcasted_iota(jnp.int32, (rows, LANES), 1)
    tiles = []
    for k in range(HEADS_PER_GROUP // 2):
        a = jnp.broadcast_to(x[:, 2 * k:2 * k + 1], (rows, LANES))
        b = jnp.broadcast_to(x[:, 2 * k + 1:2 * k + 2], (rows, LANES))
        tiles.append(jnp.where(lane < HEAD_DIM, a, b))
    return jnp.concatenate(tiles, axis=1)


def _conv_silu(xpad, r0, cw_refs, cb_refs, q):
    w = jnp.concatenate([r[...] for r in cw_refs], axis=1)
    bias = jnp.concatenate([r[...] for r in cb_refs], axis=1)
    x = xpad[pl.ds(r0 - SUBLANES, q + SUBLANES), :]
    acc = w[0:1, :] * x
    for k in range(1, CONV_K):
        acc = pltpu.roll(acc, 1, axis=0) + w[k:k + 1, :] * x
    return _silu(acc[SUBLANES:, :] + bias)


def _stage(xpad, r0, nrows, xs, b, c):
    xpad[pl.ds(r0, nrows), 0:GROUP_W] = xs
    xpad[pl.ds(r0, nrows), GROUP_W:GROUP_W + D_STATE] = b
    xpad[pl.ds(r0, nrows), GROUP_W + D_STATE:XBC_W] = c


def _stage_halo(xpad, r0, pxs, pb, pc):
    xpad[pl.ds(r0 - SUBLANES, SUBLANES), :] = jnp.zeros((SUBLANES, XBC_W), f32)
    _stage(xpad, r0 - (CONV_K - 1), CONV_K - 1, pxs, pb, pc)


def _gate_norm(y, xs, dsk_ref, zs_ref, gout_ref):
    y = y + dsk_ref[...] * xs
    y = y * zs_ref[...].astype(f32)
    ms = jnp.mean(y * y, axis=-1, keepdims=True)
    return y * lax.rsqrt(ms + SSD_NORM_EPS) * gout_ref[...]


def _side_projection(h_ref, w_bf16, act):
    return act(jnp.dot(h_ref[...], w_bf16[...], preferred_element_type=f32))


def _decay_rows(cd_b):
    return jnp.concatenate(
        [jnp.broadcast_to(cd_b[h:h + 1, :], (HEAD_DIM, D_STATE)) for h in range(HEADS_PER_GROUP)], axis=0)


def _ssd_prompt_kernel(xs_ref, b_ref, c_ref, pxs_ref, pb_ref, pc_ref,
                       cwx_ref, cwb_ref, cwc_ref, cbx_ref, cbb_ref, cbc_ref,
                       dt_ref, dtt_ref, dtb_ref, dtbt_ref, alog_ref, alogt_ref,
                       dsk_ref, zs_ref, gout_ref, h_ref, wg_ref,
                       yn_ref, ssm_ref, sg_ref, xpad, state, wg_bf16, *, nc, n_mt, side_act):
    q = CHUNK
    r0 = SUBLANES
    s = pl.program_id(0)
    c = s % nc
    hi = lax.Precision.HIGHEST

    @pl.when(s % n_mt == 0)
    def _():
        wg_bf16[...] = wg_ref[...].astype(bf16)

    @pl.when(c == 0)
    def _():
        state[...] = jnp.zeros_like(state)
        _stage_halo(xpad, r0, pxs_ref[0], pb_ref[0], pc_ref[0])

    @pl.when(c > 0)
    def _():
        xpad[pl.ds(r0 - 3, 3), :] = xpad[pl.ds(r0 + q - 3, 3), :]

    sg_ref[...] = _side_projection(h_ref, wg_bf16, side_act).astype(sg_ref.dtype)
    _stage(xpad, r0, q, xs_ref[...], b_ref[...], c_ref[...])
    act = _conv_silu(xpad, r0, (cwx_ref, cwb_ref, cwc_ref), (cbx_ref, cbb_ref, cbc_ref), q)
    xs = act[:, 0:GROUP_W]
    bm = act[:, GROUP_W:GROUP_W + D_STATE].astype(bf16)
    cm = act[:, GROUP_W + D_STATE:XBC_W].astype(bf16)

    dt = _softplus(dt_ref[0] + dtb_ref[0])
    dta = dt * (-jnp.exp(alog_ref[0]))
    dtt = _softplus(dtt_ref[0] + dtbt_ref[0])
    dtat = dtt * (-jnp.exp(alogt_ref[0]))
    row = lax.broadcasted_iota(jnp.int32, (q, q), 0)
    col = lax.broadcasted_iota(jnp.int32, (q, q), 1)
    causal = row >= col
    acum = jnp.dot(causal.astype(f32), dta, precision=hi, preferred_element_type=f32)
    acum_t = jnp.dot(dtat, (row <= col).astype(f32), precision=hi, preferred_element_type=f32)
    tot_b = jnp.dot(dtat, jnp.ones((q, D_STATE), f32), precision=hi, preferred_element_type=f32)

    dt_x = _expand_heads(dt, q)
    acum_x = _expand_heads(acum, q)
    xdt = xs * dt_x
    decay_x = jnp.exp(acum_x[q - 1:q, :] - acum_x)
    xdtd = (xdt * decay_x).astype(bf16)
    xdt_b = xdt.astype(bf16)

    cb = lax.dot_general(cm, bm, (((1,), (1,)), ((), ())), preferred_element_type=f32)
    cb = jnp.where(causal, cb, 0.0)
    lane = lax.broadcasted_iota(jnp.int32, (q, LANES), 1)
    tiles = []
    for k in range(HEADS_PER_GROUP // 2):
        res = []
        for h in (2 * k, 2 * k + 1):
            seg = jnp.broadcast_to(acum[:, h:h + 1], (q, q)) - jnp.broadcast_to(acum_t[h:h + 1, :], (q, q))
            m_h = (cb * jnp.exp(jnp.minimum(seg, 0.0))).astype(bf16)
            res.append(jnp.dot(m_h, xdt_b[:, k * LANES:(k + 1) * LANES], preferred_element_type=f32))
        tiles.append(jnp.where(lane < HEAD_DIM, res[0], res[1]))
    y_diag = jnp.concatenate(tiles, axis=1)

    s_prev = state[...]
    y_off = lax.dot_general(cm, s_prev.astype(bf16), (((1,), (1,)), ((), ())), preferred_element_type=f32)
    y_off = y_off * jnp.exp(acum_x)
    new_states = lax.dot_general(xdtd, bm, (((0,), (0,)), ((), ())), preferred_element_type=f32)
    s_new = s_prev * _decay_rows(jnp.exp(tot_b)) + new_states
    state[...] = s_new
    yn_ref[...] = _gate_norm(y_diag + y_off, xs, dsk_ref, zs_ref, gout_ref).astype(yn_ref.dtype)

    @pl.when(c == nc - 1)
    def _():
        ssm_ref[0, 0] = state[...]


def _ssd_short_chunk(xpad, r0, cw_refs, cb_refs, dt_raw, dtb, alog, s_prev, q):
    hi = lax.Precision.HIGHEST
    act = _conv_silu(xpad, r0, cw_refs, cb_refs, q)
    xs = act[:, 0:GROUP_W]
    bm = act[:, GROUP_W:GROUP_W + D_STATE].astype(bf16)
    cm = act[:, GROUP_W + D_STATE:XBC_W].astype(bf16)

    dt = _softplus(dt_raw + dtb)
    dta = dt * (-jnp.exp(alog))
    row = lax.broadcasted_iota(jnp.int32, (q, q), 0)
    col = lax.broadcasted_iota(jnp.int32, (q, q), 1)
    acum = jnp.dot((row >= col).astype(f32), dta, precision=hi, preferred_element_type=f32)
    tot_b = lax.dot_general(dta, jnp.ones((q, D_STATE), f32), (((0,), (0,)), ((), ())),
                            precision=hi, preferred_element_type=f32)

    dt_x = _expand_heads(dt, q)
    acum_x = _expand_heads(acum, q)
    xdt = xs * dt_x
    decay_x = jnp.exp(acum_x[q - 1:q, :] - acum_x)
    xdtd = (xdt * decay_x).astype(bf16)

    cb = lax.dot_general(cm, bm, (((1,), (1,)), ((), ())), preferred_element_type=f32)
    rowx = lax.broadcasted_iota(jnp.int32, (q, GROUP_W), 0)
    y = jnp.zeros((q, GROUP_W), f32)
    for j in range(q):
        wj = jnp.where(rowx >= j, jnp.exp(acum_x - acum_x[j:j + 1, :]), 0.0)
        wj = wj * jnp.broadcast_to(cb[:, j:j + 1], (q, GROUP_W))
        y = y + wj * xdt[j:j + 1, :]

    y_off = lax.dot_general(cm, s_prev.astype(bf16), (((1,), (1,)), ((), ())), preferred_element_type=f32)
    y = y + y_off * jnp.exp(acum_x)
    new_states = lax.dot_general(xdtd, bm, (((0,), (0,)), ((), ())), preferred_element_type=f32)
    return y, xs, s_prev * _decay_rows(jnp.exp(tot_b)) + new_states


def _ssd_sample_kernel(xs_ref, b_ref, c_ref, pxs_ref, pb_ref, pc_ref,
                       cwx_ref, cwb_ref, cwc_ref, cbx_ref, cbb_ref, cbc_ref,
                       dt_ref, dtb_ref, alog_ref, dsk_ref, zs_ref, gout_ref, s0_ref,
                       yn_ref, ssm_ref, xpad, *, q, nseq):
    ys, xss, states = [], [], []
    stride = 2 * SUBLANES
    for n in range(nseq):
        r0 = n * stride + SUBLANES
        rows = pl.ds(n * q, q)
        _stage_halo(xpad, r0, pxs_ref[n], pb_ref[n], pc_ref[n])
        _stage(xpad, r0, q, xs_ref[rows, :], b_ref[rows, :], c_ref[rows, :])
        y, xs, s_new = _ssd_short_chunk(xpad, r0, (cwx_ref, cwb_ref, cwc_ref), (cbx_ref, cbb_ref, cbc_ref),
                                        dt_ref[0, rows, :], dtb_ref[0], alog_ref[0], s0_ref[n, 0], q)
        states.append(s_new)
        ys.append(y)
        xss.append(xs)
    yn = _gate_norm(jnp.concatenate(ys, axis=0), jnp.concatenate(xss, axis=0), dsk_ref, zs_ref, gout_ref)
    for n in range(nseq):
        ssm_ref[n, 0] = states[n]
    yn_ref[...] = yn.astype(yn_ref.dtype)


def _group_param(v):
    return v.reshape(N_GROUPS, 1, HEADS_PER_GROUP), v.reshape(N_GROUPS, HEADS_PER_GROUP, 1)


def ssd_branch(xbc, dt_raw, zs, conv_prev, s0, conv_w, conv_b, dt_bias, a_log, d_skip, g_out,
               *, row0, nb, seq, name, side_proj=None):
    m = xbc.shape[0]
    q = min(seq, CHUNK)
    nc = seq // q
    nseq = 1 if s0 is None else SAMPLE_SEQS_PER_STEP
    rows = nseq * q
    rb0 = row0 // rows
    dtg = dt_raw.reshape(m, N_GROUPS, HEADS_PER_GROUP).transpose(1, 0, 2)
    dtb, dtbt = _group_param(dt_bias)
    alog, alogt = _group_param(a_log)
    dsk = jnp.repeat(d_skip, HEAD_DIM).reshape(1, D_INNER)
    gout = g_out.reshape(1, D_INNER)
    cb2 = conv_b.reshape(1, CONV_DIM)
    nbg, ncg = D_INNER // D_STATE, (D_INNER + N_GROUPS * D_STATE) // D_STATE

    if s0 is None:
        steps = nb * N_GROUPS * nc
        ix = lambda f: (lambda s: f(s // (N_GROUPS * nc), (s // nc) % N_GROUPS,
                                    rb0 + (s // (N_GROUPS * nc)) * nc + s % nc))
    else:
        assert nc == 1 and nb % nseq == 0
        steps = (nb // nseq) * N_GROUPS
        ix = lambda f: (lambda s: f(s // N_GROUPS, s % N_GROUPS, rb0 + s // N_GROUPS))
    grid = (steps,)
    sem = ("arbitrary",)

    in_specs = [
        pl.BlockSpec((rows, GROUP_W), ix(lambda b, g, r: (r, g))),
        pl.BlockSpec((rows, D_STATE), ix(lambda b, g, r: (r, nbg + g))),
        pl.BlockSpec((rows, D_STATE), ix(lambda b, g, r: (r, ncg + g))),
        pl.BlockSpec((nseq, CONV_K - 1, GROUP_W), ix(lambda b, g, r: (b, 0, g))),
        pl.BlockSpec((nseq, CONV_K - 1, D_STATE), ix(lambda b, g, r: (b, 0, nbg + g))),
        pl.BlockSpec((nseq, CONV_K - 1, D_STATE), ix(lambda b, g, r: (b, 0, ncg + g))),
        pl.BlockSpec((CONV_K, GROUP_W), ix(lambda b, g, r: (0, g))),
        pl.BlockSpec((CONV_K, D_STATE), ix(lambda b, g, r: (0, nbg + g))),
        pl.BlockSpec((CONV_K, D_STATE), ix(lambda b, g, r: (0, ncg + g))),
        pl.BlockSpec((1, GROUP_W), ix(lambda b, g, r: (0, g))),
        pl.BlockSpec((1, D_STATE), ix(lambda b, g, r: (0, nbg + g))),
        pl.BlockSpec((1, D_STATE), ix(lambda b, g, r: (0, ncg + g))),
        pl.BlockSpec((1, rows, HEADS_PER_GROUP), ix(lambda b, g, r: (g, r, 0))),
    ]
    args = [xbc, xbc, xbc, conv_prev, conv_prev, conv_prev, conv_w, conv_w, conv_w, cb2, cb2, cb2, dtg]
    vec16 = pl.BlockSpec((1, 1, HEADS_PER_GROUP), ix(lambda b, g, r: (g, 0, 0)))
    vec16t = pl.BlockSpec((1, HEADS_PER_GROUP, 1), ix(lambda b, g, r: (g, 0, 0)))
    tail_specs = [
        pl.BlockSpec((1, GROUP_W), ix(lambda b, g, r: (0, g))),
        pl.BlockSpec((rows, GROUP_W), ix(lambda b, g, r: (r, g))),
        pl.BlockSpec((1, GROUP_W), ix(lambda b, g, r: (0, g))),
    ]
    state_spec = pl.BlockSpec((nseq, 1, GROUP_W, D_STATE), ix(lambda b, g, r: (b, g, 0, 0)))
    yn_spec = pl.BlockSpec((rows, GROUP_W), ix(lambda b, g, r: (r - rb0, g)))
    out_shape = (jax.ShapeDtypeStruct((nb * seq, D_INNER), bf16),
                 jax.ShapeDtypeStruct((nb, N_GROUPS, GROUP_W, D_STATE), f32))

    if s0 is None:
        dtgt = dt_raw.reshape(m, N_GROUPS, HEADS_PER_GROUP).transpose(1, 2, 0)
        in_specs += [pl.BlockSpec((1, HEADS_PER_GROUP, q), ix(lambda b, g, r: (g, 0, r))),
                     vec16, vec16t, vec16, vec16t] + tail_specs
        args += [dtgt, dtb, dtbt, alog, alogt, dsk, zs, gout]
        ph, pw, pcol0, pn, pact, pdtype = side_proj
        ptn = 2 * LANES
        n_nt = pn // ptn
        n_mt = steps // n_nt
        ptm = ph.shape[0] // n_mt
        assert n_mt * n_nt == steps and ptm * n_mt == ph.shape[0] and ptm % (2 * SUBLANES) == 0
        pk = ph.shape[1]
        in_specs += [pl.BlockSpec((ptm, pk), lambda s: (s % n_mt, 0)),
                     pl.BlockSpec((pl.Element(pk), pl.Element(ptn)),
                                  lambda s: (0, pl.multiple_of(pcol0 + (s // n_mt) * ptn, LANES)))]
        args += [ph, pw]
        out_specs = (yn_spec, state_spec, pl.BlockSpec((ptm, ptn), lambda s: (s % n_mt, s // n_mt)))
        out_shape = out_shape + (jax.ShapeDtypeStruct((ph.shape[0], pn), pdtype),)
        kern = functools.partial(_ssd_prompt_kernel, nc=nc, n_mt=n_mt, side_act=pact)
        scratch = [pltpu.VMEM((SUBLANES + q, XBC_W), f32), pltpu.VMEM((GROUP_W, D_STATE), f32),
                   pltpu.VMEM((pk, ptn), bf16)]
    else:
        in_specs += [vec16, vec16] + tail_specs + [state_spec]
        args += [dtb, alog, dsk, zs, gout, s0.reshape(nb, N_GROUPS, GROUP_W, D_STATE)]
        out_specs = (yn_spec, state_spec)
        kern = functools.partial(_ssd_sample_kernel, q=q, nseq=nseq)
        scratch = [pltpu.VMEM((nseq * 2 * SUBLANES, XBC_W), f32)]

    outs = pl.pallas_call(
        kern, grid=grid, in_specs=in_specs, out_specs=out_specs, out_shape=out_shape,
        scratch_shapes=scratch, compiler_params=_params(sem), name=name,
    )(*args)
    return (outs[0], outs[1].reshape(nb, N_HEADS, HEAD_DIM, D_STATE)) + tuple(outs[2:])


def _pool_kernel(u_ref, prev_ref, o_ref, halo, *, tl, n_past, nseq, carry):
    lt = pl.program_id(1)
    base = 2 * SUBLANES
    stride = base + tl

    def init_halo(n):
        halo[pl.ds(n * stride, base - POOL_KEEP), :] = jnp.zeros((base - POOL_KEEP, halo.shape[1]), f32)
        halo[pl.ds(n * stride + base - POOL_KEEP, POOL_KEEP), :] = prev_ref[n]

    if carry:
        assert nseq == 1
        pl.when(lt == 0)(lambda: init_halo(0))

        @pl.when(lt > 0)
        def _():
            halo[pl.ds(base - POOL_KEEP, POOL_KEEP), :] = halo[pl.ds(base + tl - POOL_KEEP, POOL_KEEP), :]
    else:
        for n in range(nseq):
            init_halo(n)

    for n in range(nseq):
        halo[pl.ds(n * stride + base, tl), :] = u_ref[pl.ds(n * tl, tl), :]
    t = lt * tl + lax.broadcasted_iota(jnp.int32, (tl, 1), 0) + (n_past + 1)
    for gi, w in enumerate(POOL_WINDOWS):
        cols = slice(gi * POOL_GROUP, (gi + 1) * POOL_GROUP)
        cnt = jnp.minimum(t, w).astype(f32)
        outs = []
        for n in range(nseq):
            x = halo[pl.ds(n * stride, stride), cols]
            acc, shift = x, 1
            while shift < w:
                acc = acc + pltpu.roll(acc, shift, axis=0)
                shift *= 2
            outs.append(acc[base:, :] / cnt - x[base:, :])
        o_ref[:, cols] = jnp.concatenate(outs, axis=0).astype(o_ref.dtype)


def pool_branch(u, prev, *, row0, nb, seq, n_past, tl, name, nseq=1):
    d = u.shape[1]
    nlt = seq // tl
    assert nseq == 1 or nlt == 1
    rows = nseq * tl
    rb0 = row0 // rows
    return pl.pallas_call(
        functools.partial(_pool_kernel, tl=tl, n_past=n_past, nseq=nseq, carry=nlt > 1),
        grid=(nb // nseq, nlt),
        in_specs=[pl.BlockSpec((rows, d), lambda b, l: (rb0 + b * nlt + l, 0)),
                  pl.BlockSpec((nseq, POOL_KEEP, d), lambda b, l: (b, 0, 0))],
        out_specs=pl.BlockSpec((rows, d), lambda b, l: (b * nlt + l, 0)),
        out_shape=jax.ShapeDtypeStruct((nb * seq, d), bf16),
        scratch_shapes=[pltpu.VMEM((nseq * (2 * SUBLANES + tl), d), f32)],
        compiler_params=_params(("parallel", "arbitrary")),
        name=name,
    )(u, prev)


def _poolmix_kernel(ap_ref, as_ref, w_ref, s_ref, o_ref, *, n_p):
    def mix(a_ref):
        acc = jnp.dot(a_ref[...], w_ref[0].astype(bf16), preferred_element_type=f32)
        o_ref[...] = (acc * s_ref[...]).astype(o_ref.dtype)

    i = pl.program_id(0)
    pl.when(i < n_p)(lambda: mix(ap_ref))
    pl.when(i >= n_p)(lambda: mix(as_ref))


def pool_mix(pooled_p, pooled_s, w_mix, scale, *, tm, name):
    (mp, d), ms = pooled_p.shape, pooled_s.shape[0]
    n_p = mp // tm
    ng = d // POOL_GROUP
    return pl.pallas_call(
        functools.partial(_poolmix_kernel, n_p=n_p),
        grid=((mp + ms) // tm, ng),
        in_specs=[pl.BlockSpec((tm, POOL_GROUP), lambda i, g: (jnp.minimum(i, n_p - 1), g)),
                  pl.BlockSpec((tm, POOL_GROUP), lambda i, g: (jnp.maximum(i - n_p, 0), g)),
                  pl.BlockSpec((1, POOL_GROUP, POOL_GROUP), lambda i, g: (g, 0, 0)),
                  pl.BlockSpec((1, POOL_GROUP), lambda i, g: (0, g))],
        out_specs=pl.BlockSpec((tm, POOL_GROUP), lambda i, g: (i, g)),
        out_shape=jax.ShapeDtypeStruct((mp + ms, d), bf16),
        compiler_params=_params(("arbitrary", "arbitrary")),
        name=name,
    )(pooled_p, pooled_s, w_mix, scale.reshape(1, d))


def _ple_epilogue(acc, x2, p, wp, ssq):
    pe = jnp.dot(p, wp.astype(bf16), preferred_element_type=f32)
    return x2 + jax.nn.sigmoid(acc * _row_scale(ssq, D_MODEL, EPS)) * pe


def _last_rows(a2d, seq, keep, row0, nb):
    grp = 2 * SUBLANES
    a3 = a2d.reshape(a2d.shape[0] // grp, grp, a2d.shape[1])
    first = row0 // grp + seq // grp - 1
    last16 = lax.slice(a3, (first, 0, 0), (first + (nb - 1) * (seq // grp) + 1, grp, a2d.shape[1]),
                       (seq // grp, 1, 1))
    return last16[:, grp - keep:]


def kernel(x_prompt, x_sample, p_prompt, p_sample, state_ssm, state_conv, state_pool, g_mix, w_in, conv_w, conv_b, dt_bias, a_log, d_skip, g_ssd_out, w_ssd_out, w_pool_mix, pool_scale, w_pool_out, w_o, g_ffn, w_up, w_down, g_ple, w_ple_gate, w_ple_proj, g_final):
    bp, lp, d = x_prompt.shape
    bs, ls, _ = x_sample.shape
    mp, ms = bp * lp, bs * ls
    n_p = mp // TM
    xp2, xs2 = x_prompt.reshape(mp, d), x_sample.reshape(ms, d)
    pe = jnp.concatenate([p_prompt[0].reshape(mp, D_PLE), p_sample[0].reshape(ms, D_PLE)], axis=0).astype(bf16)

    c_xbc = D_INNER
    c_dt = c_xbc + CONV_DIM
    c_u = c_dt + N_HEADS
    c_gate = c_u + D_MODEL
    w_in0 = w_in[0]

    h = rmsnorm_stacked(xp2, xs2, g_mix[0], bf16, name="norm_mix")
    zs = matmul(h, w_in0, n=D_INNER, col0=0, tm=TM_BIG, out_dtype=bf16, epilogue=_silu, name="proj_z")
    xbc = matmul(h, w_in0, n=CONV_DIM, col0=c_xbc, tm=TM_BIG, name="proj_xbc")
    dt_raw = matmul(h, w_in0, n=N_HEADS, col0=c_dt, tm=TM_BIG, tn=N_HEADS, name="proj_dt")
    u = matmul(h, w_in0, n=D_MODEL, col0=c_u, tm=TM_BIG, name="proj_u")

    ssd_w = (conv_w[0], conv_b[0], dt_bias[0], a_log[0], d_skip[0], g_ssd_out[0])
    conv0_p = jnp.zeros((bp, CONV_K - 1, CONV_DIM), f32)
    yn_p, ssm_p, sg = ssd_branch(xbc, dt_raw, zs, conv0_p, None, *ssd_w, row0=0, nb=bp, seq=lp,
                                 name="ssd_prompt_gates",
                                 side_proj=(h, w_in0, c_gate, 2 * D_MODEL, jax.nn.sigmoid, bf16))
    yn_s, ssm_s = ssd_branch(xbc, dt_raw, zs, state_conv[0], state_ssm[0], *ssd_w, row0=mp, nb=bs, seq=ls,
                             name="ssd_sample")

    pool0_p = jnp.zeros((bp, POOL_KEEP, d), f32)
    pooled_p = pool_branch(u, pool0_p, row0=0, nb=bp, seq=lp, n_past=0, tl=256, name="pool_prompt")
    pooled_s = pool_branch(u, state_pool[0], row0=mp, nb=bs, seq=ls, n_past=PAST_LEN, tl=ls,
                           nseq=SAMPLE_SEQS_PER_STEP, name="pool_sample")
    pm = pool_mix(pooled_p, pooled_s, w_pool_mix[0], pool_scale[0], tm=TM, name="pool_mix")

    kt_rows = dict(KT, tm=TM)
    ya_p = matmul(yn_p, w_ssd_out[0], **kt_rows, name="ssd_out_prompt")
    ya_s = matmul(yn_s, w_ssd_out[0], **kt_rows, name="ssd_out_sample")
    merged = matmul(pm, w_pool_out[0], out_dtype=bf16, name="pool_out_merge",
                    epilogue=lambda yb, ga, gb, ya_tp, ya_ts: (ga.astype(f32) * _pick_stacked(ya_tp, ya_ts, n_p)
                                                               + gb.astype(f32) * yb),
                    extras=[_tile(sg), (sg, (TM, TN), lambda i, j, k: (i, j + D_MODEL // TN))]
                    + _stacked_tiles(ya_p, ya_s))
    x1, x1g, ssq1 = matmul(merged, w_o[0], name="out_proj", norm_gain=g_ffn[0],
                           epilogue=lambda acc, r_p, r_s: _pick_stacked(r_p, r_s, n_p) + acc,
                           extras=_stacked_tiles(xp2, xs2))

    act = matmul(x1g, w_up[0], tm=TM_BIG, out_dtype=bf16, name="mlp_up",
                 epilogue=lambda acc, ssq: jnp.square(jnp.maximum(acc * _row_scale(ssq, D_MODEL, EPS), 0.0)),
                 extras=[_ssq_tiles(ssq1, TM_BIG)])
    x2, x2g, ssq2 = matmul(act, w_down[0], **KT, name="mlp_down", norm_gain=g_ple[0],
                           extras=[_tile(x1, KT["tm"], KT["tn"])])

    x3 = matmul(x2g, w_ple_gate[0], name="ple", epilogue=_ple_epilogue,
                extras=[_tile(x2),
                        (pe, (TM, D_PLE), lambda i, j, k: (i, 0)),
                        (w_ple_proj[0], (D_PLE, TN), lambda i, j, k: (0, j)),
                        _ssq_tiles(ssq2, TM)])

    y_p = rmsnorm(x3, g_final, f32, rows=mp, name="norm_final_prompt").reshape(bp, lp, d)
    y_s = rmsnorm(x3, g_final, f32, row_block0=mp // 512, rows=ms, name="norm_final_sample").reshape(bs, ls, d)

    conv_p = _last_rows(xbc, lp, CONV_K - 1, 0, bp)
    pool_p = _last_rows(u, lp, POOL_KEEP, 0, bp)
    xbc_s = xbc.reshape(-1, SUBLANES, CONV_DIM)[mp // SUBLANES:]
    u_s = u.reshape(-1, SUBLANES, d)[mp // SUBLANES:]
    conv_s = xbc_s[:, ls - (CONV_K - 1):]
    pool_s = jnp.concatenate([state_pool[0][:, ls:], u_s], axis=1)
    return (y_p, y_s, ssm_p[None], conv_p[None], pool_p[None], ssm_s[None], conv_s[None], pool_s[None])
```

```python
import functools

import jax
import jax.numpy as jnp
from jax import lax
from jax.experimental import pallas as pl
from jax.experimental.pallas import tpu as pltpu

f32 = jnp.float32
bf16 = jnp.bfloat16

D_MODEL = 4096
D_INNER = 8192
HEAD_DIM = 64
N_HEADS = 128
N_GROUPS = 8
HEADS_PER_GROUP = N_HEADS // N_GROUPS
GROUP_W = D_INNER // N_GROUPS
D_STATE = 128
CONV_K = 4
CONV_DIM = D_INNER + 2 * N_GROUPS * D_STATE
XBC_W = GROUP_W + 2 * D_STATE
POOL_WINDOWS = (2, 4, 8, 16)
POOL_GROUP = 1024
POOL_KEEP = 15
D_PLE = 256
EPS = 1e-6
SSD_NORM_EPS = 1e-5
CHUNK = 128
PAST_LEN = 16384

LANES = 128
SUBLANES = 8
VMEM_LIMIT = 56 * 1024 * 1024
TM = 1024
TM_BIG = 1536
TN = 512
KT = dict(tm=TM_BIG, tn=1024, tk=1024)
SAMPLE_SEQS_PER_STEP = 16


def _params(sem):
    return pltpu.CompilerParams(dimension_semantics=sem, vmem_limit_bytes=VMEM_LIMIT)


def _rms(x, g, eps):
    ms = jnp.mean(x * x, axis=-1, keepdims=True)
    return x * lax.rsqrt(ms + eps) * g


def _rmsnorm_kernel(x_ref, g_ref, o_ref, *, eps):
    o_ref[...] = _rms(x_ref[...], g_ref[...], eps).astype(o_ref.dtype)


def rmsnorm(x, g, out_dtype, *, name, tm=512, row_block0=0, rows=None):
    m, d = x.shape
    rows = m if rows is None else rows
    return pl.pallas_call(
        functools.partial(_rmsnorm_kernel, eps=EPS),
        grid=(rows // tm,),
        in_specs=[pl.BlockSpec((tm, d), lambda i: (i + row_block0, 0)),
                  pl.BlockSpec((1, d), lambda i: (0, 0))],
        out_specs=pl.BlockSpec((tm, d), lambda i: (i, 0)),
        out_shape=jax.ShapeDtypeStruct((rows, d), out_dtype),
        compiler_params=_params(("parallel",)),
        name=name,
    )(x, g.reshape(1, d))


def _rmsnorm2_kernel(xp_ref, xs_ref, g_ref, o_ref, *, eps, n_p):
    i = pl.program_id(0)

    @pl.when(i < n_p)
    def _():
        o_ref[...] = _rms(xp_ref[...], g_ref[...], eps).astype(o_ref.dtype)

    @pl.when(i >= n_p)
    def _():
        o_ref[...] = _rms(xs_ref[...], g_ref[...], eps).astype(o_ref.dtype)


def rmsnorm_stacked(x_p, x_s, g, out_dtype, *, name, tm=512):
    (mp, d), ms = x_p.shape, x_s.shape[0]
    n_p = mp // tm
    return pl.pallas_call(
        functools.partial(_rmsnorm2_kernel, eps=EPS, n_p=n_p),
        grid=((mp + ms) // tm,),
        in_specs=[pl.BlockSpec((tm, d), lambda i: (jnp.minimum(i, n_p - 1), 0)),
                  pl.BlockSpec((tm, d), lambda i: (jnp.maximum(i - n_p, 0), 0)),
                  pl.BlockSpec((1, d), lambda i: (0, 0))],
        out_specs=pl.BlockSpec((tm, d), lambda i: (i, 0)),
        out_shape=jax.ShapeDtypeStruct((mp + ms, d), out_dtype),
        compiler_params=_params(("arbitrary",)),
        name=name,
    )(x_p, x_s, g.reshape(1, d))


def _norm_parts(x, gain, hg_ref, ssq_ref):
    hg_ref[...] = (x * gain).astype(hg_ref.dtype)
    part = jnp.broadcast_to(jnp.sum(x * x, axis=-1, keepdims=True), ssq_ref.shape)
    j = pl.program_id(1)

    @pl.when(j == 0)
    def _():
        ssq_ref[...] = part

    @pl.when(j > 0)
    def _():
        ssq_ref[...] += part


def _mm_kernel(*refs, nk, n_extra, epilogue, with_norm):
    a_ref, w_ref = refs[0], refs[1]
    extras = refs[2:2 + n_extra]
    gain_ref = refs[2 + n_extra] if with_norm else None
    o_ref = refs[2 + n_extra + with_norm]
    norm_refs = refs[3 + n_extra + with_norm:]

    def product():
        return jnp.dot(a_ref[...], w_ref[...].astype(bf16), preferred_element_type=f32)

    if nk == 1:
        out = epilogue(product(), *[e[...] for e in extras])
        o_ref[...] = out.astype(o_ref.dtype)
        if with_norm:
            _norm_parts(out, gain_ref[...], *norm_refs)
    else:
        k = pl.program_id(2)

        @pl.when(k == 0)
        def _():
            o_ref[...] = extras[0][...] + product() if n_extra else product()

        @pl.when(k > 0)
        def _():
            o_ref[...] += product()

        if with_norm:
            @pl.when(k == nk - 1)
            def _():
                _norm_parts(o_ref[...], gain_ref[...], *norm_refs)


def matmul(a, w, *, name, n=None, col0=0, tm=TM, tn=TN, tk=None, out_dtype=f32, epilogue=None, extras=(),
           norm_gain=None):
    m, kdim = a.shape
    n = w.shape[1] if n is None else n
    tk = kdim if tk is None else tk
    nk = kdim // tk
    if nk > 1:
        assert epilogue is None and out_dtype == f32 and len(extras) <= 1
    if epilogue is None:
        epilogue = lambda acc: acc
    if col0 % tn == 0:
        w_spec = pl.BlockSpec((tk, tn), lambda i, j, k: (k, col0 // tn + j))
    else:
        w_spec = pl.BlockSpec((pl.Element(tk), pl.Element(tn)),
                              lambda i, j, k: (k * tk, pl.multiple_of(col0 + j * tn, LANES)))
    in_specs = [pl.BlockSpec((tm, tk), lambda i, j, k: (i, k)), w_spec]
    in_specs += [pl.BlockSpec(blk, imap) for (_, blk, imap) in extras]
    args = [a, w] + [e[0] for e in extras]
    out_specs = pl.BlockSpec((tm, tn), lambda i, j, k: (i, j))
    out_shape = jax.ShapeDtypeStruct((m, n), out_dtype)
    if norm_gain is not None:
        in_specs.append(pl.BlockSpec((1, tn), lambda i, j, k: (0, j)))
        args.append(norm_gain.reshape(1, n))
        out_specs = (out_specs, pl.BlockSpec((tm, tn), lambda i, j, k: (i, j)),
                     pl.BlockSpec((tm, LANES), lambda i, j, k: (i, 0)))
        out_shape = (out_shape, jax.ShapeDtypeStruct((m, n), bf16), jax.ShapeDtypeStruct((m, LANES), f32))
    col_sem = "parallel" if norm_gain is None else "arbitrary"
    return pl.pallas_call(
        functools.partial(_mm_kernel, nk=nk, n_extra=len(extras), epilogue=epilogue,
                          with_norm=norm_gain is not None),
        grid=(m // tm, n // tn, nk),
        in_specs=in_specs,
        out_specs=out_specs,
        out_shape=out_shape,
        compiler_params=_params(("parallel", col_sem, "arbitrary")),
        name=name,
    )(*args)


def _row_scale(ssq, dim, eps):
    return lax.rsqrt(ssq[:, 0:1] * (1.0 / dim) + eps)


def _ssq_tiles(ssq, tm):
    return (ssq, (tm, LANES), lambda i, j, k: (i, 0))


def _tile(arr, tm=TM, tn=TN):
    return (arr, (tm, tn), lambda i, j, k: (i, j))


def _stacked_tiles(x_p, x_s, tm=TM, tn=TN):
    n_p = x_p.shape[0] // tm
    return [(x_p, (tm, tn), lambda i, j, k: (jnp.minimum(i, n_p - 1), j)),
            (x_s, (tm, tn), lambda i, j, k: (jnp.maximum(i - n_p, 0), j))]


def _pick_stacked(t_p, t_s, n_p):
    return jnp.where(pl.program_id(0) < n_p, t_p, t_s)


def _silu(x):
    return x * jax.nn.sigmoid(x)


def _softplus(x):
    return jnp.maximum(x, 0.0) + jnp.log1p(jnp.exp(-jnp.abs(x)))


def _expand_heads(x, rows):
    lane = lax.broadcasted_iota(jnp.int32, (rows, LANES), 1)
    tiles = []
    for k in range(HEADS_PER_GROUP // 2):
        a = jnp.broadcast_to(x[:, 2 * k:2 * k + 1], (rows, LANES))
        b = jnp.broadcast_to(x[:, 2 * k + 1:2 * k + 2], (rows, LANES))
        tiles.append(jnp.where(lane < HEAD_DIM, a, b))
    return jnp.concatenate(tiles, axis=1)


def _conv_silu(xpad, r0, cw_refs, cb_refs, q):
    w = jnp.concatenate([r[...] for r in cw_refs], axis=1)
    bias = jnp.concatenate([r[...] for r in cb_refs], axis=1)
    x = xpad[pl.ds(r0 - SUBLANES, q + SUBLANES), :]
    acc = w[0:1, :] * x
    for k in range(1, CONV_K):
        acc = pltpu.roll(acc, 1, axis=0) + w[k:k + 1, :] * x
    return _silu(acc[SUBLANES:, :] + bias)


def _stage(xpad, r0, nrows, xs, b, c):
    xpad[pl.ds(r0, nrows), 0:GROUP_W] = xs
    xpad[pl.ds(r0, nrows), GROUP_W:GROUP_W + D_STATE] = b
    xpad[pl.ds(r0, nrows), GROUP_W + D_STATE:XBC_W] = c


def _stage_halo(xpad, r0, pxs, pb, pc):
    xpad[pl.ds(r0 - SUBLANES, SUBLANES), :] = jnp.zeros((SUBLANES, XBC_W), f32)
    _stage(xpad, r0 - (CONV_K - 1), CONV_K - 1, pxs, pb, pc)


def _gate_norm(y, xs, dsk_ref, zs_ref, gout_ref):
    y = y + dsk_ref[...] * xs
    y = y * zs_ref[...].astype(f32)
    ms = jnp.mean(y * y, axis=-1, keepdims=True)
    return y * lax.rsqrt(ms + SSD_NORM_EPS) * gout_ref[...]


def _side_projection(h_ref, w_bf16, act):
    return act(jnp.dot(h_ref[...], w_bf16[...], preferred_element_type=f32))


def _decay_rows(cd_b):
    return jnp.concatenate(
        [jnp.broadcast_to(cd_b[h:h + 1, :], (HEAD_DIM, D_STATE)) for h in range(HEADS_PER_GROUP)], axis=0)


def _ssd_prompt_kernel(xs_ref, b_ref, c_ref, pxs_ref, pb_ref, pc_ref,
                       cwx_ref, cwb_ref, cwc_ref, cbx_ref, cbb_ref, cbc_ref,
                       dt_ref, dtt_ref, dtb_ref, dtbt_ref, alog_ref, alogt_ref,
                       dsk_ref, zs_ref, gout_ref, h_ref, wg_ref,
                       yn_ref, ssm_ref, sg_ref, xpad, state, wg_bf16, *, nc, n_mt, side_act):
    q = CHUNK
    r0 = SUBLANES
    s = pl.program_id(0)
    c = s % nc
    hi = lax.Precision.HIGHEST

    @pl.when(s % n_mt == 0)
    def _():
        wg_bf16[...] = wg_ref[...].astype(bf16)

    @pl.when(c == 0)
    def _():
        state[...] = jnp.zeros_like(state)
        _stage_halo(xpad, r0, pxs_ref[0], pb_ref[0], pc_ref[0])

    @pl.when(c > 0)
    def _():
        xpad[pl.ds(r0 - 3, 3), :] = xpad[pl.ds(r0 + q - 3, 3), :]

    sg_ref[...] = _side_projection(h_ref, wg_bf16, side_act).astype(sg_ref.dtype)
    _stage(xpad, r0, q, xs_ref[...], b_ref[...], c_ref[...])
    act = _conv_silu(xpad, r0, (cwx_ref, cwb_ref, cwc_ref), (cbx_ref, cbb_ref, cbc_ref), q)
    xs = act[:, 0:GROUP_W]
    bm = act[:, GROUP_W:GROUP_W + D_STATE].astype(bf16)
    cm = act[:, GROUP_W + D_STATE:XBC_W].astype(bf16)

    dt = _softplus(dt_ref[0] + dtb_ref[0])
    dta = dt * (-jnp.exp(alog_ref[0]))
    dtt = _softplus(dtt_ref[0] + dtbt_ref[0])
    dtat = dtt * (-jnp.exp(alogt_ref[0]))
    row = lax.broadcasted_iota(jnp.int32, (q, q), 0)
    col = lax.broadcasted_iota(jnp.int32, (q, q), 1)
    causal = row >= col
    acum = jnp.dot(causal.astype(f32), dta, precision=hi, preferred_element_type=f32)
    acum_t = jnp.dot(dtat, (row <= col).astype(f32), precision=hi, preferred_element_type=f32)
    tot_b = jnp.dot(dtat, jnp.ones((q, D_STATE), f32), precision=hi, preferred_element_type=f32)

    dt_x = _expand_heads(dt, q)
    acum_x = _expand_heads(acum, q)
    xdt = xs * dt_x
    decay_x = jnp.exp(acum_x[q - 1:q, :] - acum_x)
    xdtd = (xdt * decay_x).astype(bf16)
    xdt_b = xdt.astype(bf16)

    cb = lax.dot_general(cm, bm, (((1,), (1,)), ((), ())), preferred_element_type=f32)
    cb = jnp.where(causal, cb, 0.0)
    lane = lax.broadcasted_iota(jnp.int32, (q, LANES), 1)
    tiles = []
    for k in range(HEADS_PER_GROUP // 2):
        res = []
        for h in (2 * k, 2 * k + 1):
            seg = jnp.broadcast_to(acum[:, h:h + 1], (q, q)) - jnp.broadcast_to(acum_t[h:h + 1, :], (q, q))
            m_h = (cb * jnp.exp(jnp.minimum(seg, 0.0))).astype(bf16)
            res.append(jnp.dot(m_h, xdt_b[:, k * LANES:(k + 1) * LANES], preferred_element_type=f32))
        tiles.append(jnp.where(lane < HEAD_DIM, res[0], res[1]))
    y_diag = jnp.concatenate(tiles, axis=1)

    s_prev = state[...]
    y_off = lax.dot_general(cm, s_prev.astype(bf16), (((1,), (1,)), ((), ())), preferred_element_type=f32)
    y_off = y_off * jnp.exp(acum_x)
    new_states = lax.dot_general(xdtd, bm, (((0,), (0,)), ((), ())), preferred_element_type=f32)
    s_new = s_prev * _decay_rows(jnp.exp(tot_b)) + new_states
    state[...] = s_new
    yn_ref[...] = _gate_norm(y_diag + y_off, xs, dsk_ref, zs_ref, gout_ref).astype(yn_ref.dtype)

    @pl.when(c == nc - 1)
    def _():
        ssm_ref[0, 0] = state[...]


def _ssd_short_chunk(xpad, r0, cw_refs, cb_refs, dt_raw, dtb, alog, s_prev, q):
    hi = lax.Precision.HIGHEST
    act = _conv_silu(xpad, r0, cw_refs, cb_refs, q)
    xs = act[:, 0:GROUP_W]
    bm = act[:, GROUP_W:GROUP_W + D_STATE].astype(bf16)
    cm = act[:, GROUP_W + D_STATE:XBC_W].astype(bf16)

    dt = _softplus(dt_raw + dtb)
    dta = dt * (-jnp.exp(alog))
    row = lax.broadcasted_iota(jnp.int32, (q, q), 0)
    col = lax.broadcasted_iota(jnp.int32, (q, q), 1)
    acum = jnp.dot((row >= col).astype(f32), dta, precision=hi, preferred_element_type=f32)
    tot_b = lax.dot_general(dta, jnp.ones((q, D_STATE), f32), (((0,), (0,)), ((), ())),
                            precision=hi, preferred_element_type=f32)

    dt_x = _expand_heads(dt, q)
    acum_x = _expand_heads(acum, q)
    xdt = xs * dt_x
    decay_x = jnp.exp(acum_x[q - 1:q, :] - acum_x)
    xdtd = (xdt * decay_x).astype(bf16)

    cb = lax.dot_general(cm, bm, (((1,), (1,)), ((), ())), preferred_element_type=f32)
    rowx = lax.broadcasted_iota(jnp.int32, (q, GROUP_W), 0)
    y = jnp.zeros((q, GROUP_W), f32)
    for j in range(q):
        wj = jnp.where(rowx >= j, jnp.exp(acum_x - acum_x[j:j + 1, :]), 0.0)
        wj = wj * jnp.broadcast_to(cb[:, j:j + 1], (q, GROUP_W))
        y = y + wj * xdt[j:j + 1, :]

    y_off = lax.dot_general(cm, s_prev.astype(bf16), (((1,), (1,)), ((), ())), preferred_element_type=f32)
    y = y + y_off * jnp.exp(acum_x)
    new_states = lax.dot_general(xdtd, bm, (((0,), (0,)), ((), ())), preferred_element_type=f32)
    return y, xs, s_prev * _decay_rows(jnp.exp(tot_b)) + new_states


def _ssd_sample_kernel(xs_ref, b_ref, c_ref, pxs_ref, pb_ref, pc_ref,
                       cwx_ref, cwb_ref, cwc_ref, cbx_ref, cbb_ref, cbc_ref,
                       dt_ref, dtb_ref, alog_ref, dsk_ref, zs_ref, gout_ref, s0_ref,
                       yn_ref, ssm_ref, xpad, *, q, nseq):
    ys, xss, states = [], [], []
    stride = 2 * SUBLANES
    for n in range(nseq):
        r0 = n * stride + SUBLANES
        rows = pl.ds(n * q, q)
        _stage_halo(xpad, r0, pxs_ref[n], pb_ref[n], pc_ref[n])
        _stage(xpad, r0, q, xs_ref[rows, :], b_ref[rows, :], c_ref[rows, :])
        y, xs, s_new = _ssd_short_chunk(xpad, r0, (cwx_ref, cwb_ref, cwc_ref), (cbx_ref, cbb_ref, cbc_ref),
                                        dt_ref[0, rows, :], dtb_ref[0], alog_ref[0], s0_ref[n, 0], q)
        states.append(s_new)
        ys.append(y)
        xss.append(xs)
    yn = _gate_norm(jnp.concatenate(ys, axis=0), jnp.concatenate(xss, axis=0), dsk_ref, zs_ref, gout_ref)
    for n in range(nseq):
        ssm_ref[n, 0] = states[n]
    yn_ref[...] = yn.astype(yn_ref.dtype)


def _group_param(v):
    return v.reshape(N_GROUPS, 1, HEADS_PER_GROUP), v.reshape(N_GROUPS, HEADS_PER_GROUP, 1)


def ssd_branch(xbc, dt_raw, zs, conv_prev, s0, conv_w, conv_b, dt_bias, a_log, d_skip, g_out,
               *, row0, nb, seq, name, side_proj=None):
    m = xbc.shape[0]
    q = min(seq, CHUNK)
    nc = seq // q
    nseq = 1 if s0 is None else SAMPLE_SEQS_PER_STEP
    rows = nseq * q
    rb0 = row0 // rows
    dtg = dt_raw.reshape(m, N_GROUPS, HEADS_PER_GROUP).transpose(1, 0, 2)
    dtb, dtbt = _group_param(dt_bias)
    alog, alogt = _group_param(a_log)
    dsk = jnp.repeat(d_skip, HEAD_DIM).reshape(1, D_INNER)
    gout = g_out.reshape(1, D_INNER)
    cb2 = conv_b.reshape(1, CONV_DIM)
    nbg, ncg = D_INNER // D_STATE, (D_INNER + N_GROUPS * D_STATE) // D_STATE

    if s0 is None:
        steps = nb * N_GROUPS * nc
        ix = lambda f: (lambda s: f(s // (N_GROUPS * nc), (s // nc) % N_GROUPS,
                                    rb0 + (s // (N_GROUPS * nc)) * nc + s % nc))
    else:
        assert nc == 1 and nb % nseq == 0
        steps = (nb // nseq) * N_GROUPS
        ix = lambda f: (lambda s: f(s // N_GROUPS, s % N_GROUPS, rb0 + s // N_GROUPS))
    grid = (steps,)
    sem = ("arbitrary",)

    in_specs = [
        pl.BlockSpec((rows, GROUP_W), ix(lambda b, g, r: (r, g))),
        pl.BlockSpec((rows, D_STATE), ix(lambda b, g, r: (r, nbg + g))),
        pl.BlockSpec((rows, D_STATE), ix(lambda b, g, r: (r, ncg + g))),
        pl.BlockSpec((nseq, CONV_K - 1, GROUP_W), ix(lambda b, g, r: (b, 0, g))),
        pl.BlockSpec((nseq, CONV_K - 1, D_STATE), ix(lambda b, g, r: (b, 0, nbg + g))),
        pl.BlockSpec((nseq, CONV_K - 1, D_STATE), ix(lambda b, g, r: (b, 0, ncg + g))),
        pl.BlockSpec((CONV_K, GROUP_W), ix(lambda b, g, r: (0, g))),
        pl.BlockSpec((CONV_K, D_STATE), ix(lambda b, g, r: (0, nbg + g))),
        pl.BlockSpec((CONV_K, D_STATE), ix(lambda b, g, r: (0, ncg + g))),
        pl.BlockSpec((1, GROUP_W), ix(lambda b, g, r: (0, g))),
        pl.BlockSpec((1, D_STATE), ix(lambda b, g, r: (0, nbg + g))),
        pl.BlockSpec((1, D_STATE), ix(lambda b, g, r: (0, ncg + g))),
        pl.BlockSpec((1, rows, HEADS_PER_GROUP), ix(lambda b, g, r: (g, r, 0))),
    ]
    args = [xbc, xbc, xbc, conv_prev, conv_prev, conv_prev, conv_w, conv_w, conv_w, cb2, cb2, cb2, dtg]
    vec16 = pl.BlockSpec((1, 1, HEADS_PER_GROUP), ix(lambda b, g, r: (g, 0, 0)))
    vec16t = pl.BlockSpec((1, HEADS_PER_GROUP, 1), ix(lambda b, g, r: (g, 0, 0)))
    tail_specs = [
        pl.BlockSpec((1, GROUP_W), ix(lambda b, g, r: (0, g))),
        pl.BlockSpec((rows, GROUP_W), ix(lambda b, g, r: (r, g))),
        pl.BlockSpec((1, GROUP_W), ix(lambda b, g, r: (0, g))),
    ]
    state_spec = pl.BlockSpec((nseq, 1, GROUP_W, D_STATE), ix(lambda b, g, r: (b, g, 0, 0)))
    yn_spec = pl.BlockSpec((rows, GROUP_W), ix(lambda b, g, r: (r - rb0, g)))
    out_shape = (jax.ShapeDtypeStruct((nb * seq, D_INNER), bf16),
                 jax.ShapeDtypeStruct((nb, N_GROUPS, GROUP_W, D_STATE), f32))

    if s0 is None:
        dtgt = dt_raw.reshape(m, N_GROUPS, HEADS_PER_GROUP).transpose(1, 2, 0)
        in_specs += [pl.BlockSpec((1, HEADS_PER_GROUP, q), ix(lambda b, g, r: (g, 0, r))),
                     vec16, vec16t, vec16, vec16t] + tail_specs
        args += [dtgt, dtb, dtbt, alog, alogt, dsk, zs, gout]
        ph, pw, pcol0, pn, pact, pdtype = side_proj
        ptn = 2 * LANES
        n_nt = pn // ptn
        n_mt = steps // n_nt
        ptm = ph.shape[0] // n_mt
        assert n_mt * n_nt == steps and ptm * n_mt == ph.shape[0] and ptm % (2 * SUBLANES) == 0
        pk = ph.shape[1]
        in_specs += [pl.BlockSpec((ptm, pk), lambda s: (s % n_mt, 0)),
                     pl.BlockSpec((pl.Element(pk), pl.Element(ptn)),
                                  lambda s: (0, pl.multiple_of(pcol0 + (s // n_mt) * ptn, LANES)))]
        args += [ph, pw]
        out_specs = (yn_spec, state_spec, pl.BlockSpec((ptm, ptn), lambda s: (s % n_mt, s // n_mt)))
        out_shape = out_shape + (jax.ShapeDtypeStruct((ph.shape[0], pn), pdtype),)
        kern = functools.partial(_ssd_prompt_kernel, nc=nc, n_mt=n_mt, side_act=pact)
        scratch = [pltpu.VMEM((SUBLANES + q, XBC_W), f32), pltpu.VMEM((GROUP_W, D_STATE), f32),
                   pltpu.VMEM((pk, ptn), bf16)]
    else:
        in_specs += [vec16, vec16] + tail_specs + [state_spec]
        args += [dtb, alog, dsk, zs, gout, s0.reshape(nb, N_GROUPS, GROUP_W, D_STATE)]
        out_specs = (yn_spec, state_spec)
        kern = functools.partial(_ssd_sample_kernel, q=q, nseq=nseq)
        scratch = [pltpu.VMEM((nseq * 2 * SUBLANES, XBC_W), f32)]

    outs = pl.pallas_call(
        kern, grid=grid, in_specs=in_specs, out_specs=out_specs, out_shape=out_shape,
        scratch_shapes=scratch, compiler_params=_params(sem), name=name,
    )(*args)
    return (outs[0], outs[1].reshape(nb, N_HEADS, HEAD_DIM, D_STATE)) + tuple(outs[2:])


def _pool_kernel(u_ref, prev_ref, o_ref, halo, *, tl, n_past, nseq, carry):
    lt = pl.program_id(1)
    base = 2 * SUBLANES
    stride = base + tl

    def init_halo(n):
        halo[pl.ds(n * stride, base - POOL_KEEP), :] = jnp.zeros((base - POOL_KEEP, halo.shape[1]), f32)
        halo[pl.ds(n * stride + base - POOL_KEEP, POOL_KEEP), :] = prev_ref[n]

    if carry:
        assert nseq == 1
        pl.when(lt == 0)(lambda: init_halo(0))

        @pl.when(lt > 0)
        def _():
            halo[pl.ds(base - POOL_KEEP, POOL_KEEP), :] = halo[pl.ds(base + tl - POOL_KEEP, POOL_KEEP), :]
    else:
        for n in range(nseq):
            init_halo(n)

    for n in range(nseq):
        halo[pl.ds(n * stride + base, tl), :] = u_ref[pl.ds(n * tl, tl), :]
    t = lt * tl + lax.broadcasted_iota(jnp.int32, (tl, 1), 0) + (n_past + 1)
    for gi, w in enumerate(POOL_WINDOWS):
        cols = slice(gi * POOL_GROUP, (gi + 1) * POOL_GROUP)
        cnt = jnp.minimum(t, w).astype(f32)
        outs = []
        for n in range(nseq):
            x = halo[pl.ds(n * stride, stride), cols]
            acc, shift = x, 1
            while shift < w:
                acc = acc + pltpu.roll(acc, shift, axis=0)
                shift *= 2
            outs.append(acc[base:, :] / cnt - x[base:, :])
        o_ref[:, cols] = jnp.concatenate(outs, axis=0).astype(o_ref.dtype)


def pool_branch(u, prev, *, row0, nb, seq, n_past, tl, name, nseq=1):
    d = u.shape[1]
    nlt = seq // tl
    assert nseq == 1 or nlt == 1
    rows = nseq * tl
    rb0 = row0 // rows
    return pl.pallas_call(
        functools.partial(_pool_kernel, tl=tl, n_past=n_past, nseq=nseq, carry=nlt > 1),
        grid=(nb // nseq, nlt),
        in_specs=[pl.BlockSpec((rows, d), lambda b, l: (rb0 + b * nlt + l, 0)),
                  pl.BlockSpec((nseq, POOL_KEEP, d), lambda b, l: (b, 0, 0))],
        out_specs=pl.BlockSpec((rows, d), lambda b, l: (b * nlt + l, 0)),
        out_shape=jax.ShapeDtypeStruct((nb * seq, d), bf16),
        scratch_shapes=[pltpu.VMEM((nseq * (2 * SUBLANES + tl), d), f32)],
        compiler_params=_params(("parallel", "arbitrary")),
        name=name,
    )(u, prev)


def _poolmix_kernel(ap_ref, as_ref, w_ref, s_ref, o_ref, *, n_p):
    def mix(a_ref):
        acc = jnp.dot(a_ref[...], w_ref[0].astype(bf16), preferred_element_type=f32)
        o_ref[...] = (acc * s_ref[...]).astype(o_ref.dtype)

    i = pl.program_id(0)
    pl.when(i < n_p)(lambda: mix(ap_ref))
    pl.when(i >= n_p)(lambda: mix(as_ref))


def pool_mix(pooled_p, pooled_s, w_mix, scale, *, tm, name):
    (mp, d), ms = pooled_p.shape, pooled_s.shape[0]
    n_p = mp // tm
    ng = d // POOL_GROUP
    return pl.pallas_call(
        functools.partial(_poolmix_kernel, n_p=n_p),
        grid=((mp + ms) // tm, ng),
        in_specs=[pl.BlockSpec((tm, POOL_GROUP), lambda i, g: (jnp.minimum(i, n_p - 1), g)),
                  pl.BlockSpec((tm, POOL_GROUP), lambda i, g: (jnp.maximum(i - n_p, 0), g)),
                  pl.BlockSpec((1, POOL_GROUP, POOL_GROUP), lambda i, g: (g, 0, 0)),
                  pl.BlockSpec((1, POOL_GROUP), lambda i, g: (0, g))],
        out_specs=pl.BlockSpec((tm, POOL_GROUP), lambda i, g: (i, g)),
        out_shape=jax.ShapeDtypeStruct((mp + ms, d), bf16),
        compiler_params=_params(("arbitrary", "arbitrary")),
        name=name,
    )(pooled_p, pooled_s, w_mix, scale.reshape(1, d))


def _ple_epilogue(acc, x2, p, wp, ssq):
    pe = jnp.dot(p, wp.astype(bf16), preferred_element_type=f32)
    return x2 + jax.nn.sigmoid(acc * _row_scale(ssq, D_MODEL, EPS)) * pe


def _last_rows(a2d, seq, keep, row0, nb):
    grp = 2 * SUBLANES
    a3 = a2d.reshape(a2d.shape[0] // grp, grp, a2d.shape[1])
    first = row0 // grp + seq // grp - 1
    last16 = lax.slice(a3, (first, 0, 0), (first + (nb - 1) * (seq // grp) + 1, grp, a2d.shape[1]),
                       (seq // grp, 1, 1))
    return last16[:, grp - keep:]


def kernel(x_prompt, x_sample, p_prompt, p_sample, state_ssm, state_conv, state_pool, g_mix, w_in, conv_w, conv_b, dt_bias, a_log, d_skip, g_ssd_out, w_ssd_out, w_pool_mix, pool_scale, w_pool_out, w_o, g_ffn, w_up, w_down, g_ple, w_ple_gate, w_ple_proj, g_final):
    bp, lp, d = x_prompt.shape
    bs, ls, _ = x_sample.shape
    mp, ms = bp * lp, bs * ls
    n_p = mp // TM
    xp2, xs2 = x_prompt.reshape(mp, d), x_sample.reshape(ms, d)
    pe = jnp.concatenate([p_prompt[0].reshape(mp, D_PLE), p_sample[0].reshape(ms, D_PLE)], axis=0).astype(bf16)

    c_xbc = D_INNER
    c_dt = c_xbc + CONV_DIM
    c_u = c_dt + N_HEADS
    c_gate = c_u + D_MODEL
    w_in0 = w_in[0]

    h = rmsnorm_stacked(xp2, xs2, g_mix[0], bf16, name="norm_mix")
    zs = matmul(h, w_in0, n=D_INNER, col0=0, tm=TM_BIG, out_dtype=bf16, epilogue=_silu, name="proj_z")
    xbc = matmul(h, w_in0, n=CONV_DIM, col0=c_xbc, tm=TM_BIG, name="proj_xbc")
    dt_raw = matmul(h, w_in0, n=N_HEADS, col0=c_dt, tm=TM_BIG, tn=N_HEADS, name="proj_dt")
    u = matmul(h, w_in0, n=D_MODEL, col0=c_u, tm=TM_BIG, name="proj_u")

    ssd_w = (conv_w[0], conv_b[0], dt_bias[0], a_log[0], d_skip[0], g_ssd_out[0])
    conv0_p = jnp.zeros((bp, CONV_K - 1, CONV_DIM), f32)
    yn_p, ssm_p, sg = ssd_branch(xbc, dt_raw, zs, conv0_p, None, *ssd_w, row0=0, nb=bp, seq=lp,
                                 name="ssd_prompt_gates",
                                 side_proj=(h, w_in0, c_gate, 2 * D_MODEL, jax.nn.sigmoid, bf16))
    yn_s, ssm_s = ssd_branch(xbc, dt_raw, zs, state_conv[0], state_ssm[0], *ssd_w, row0=mp, nb=bs, seq=ls,
                             name="ssd_sample")

    pool0_p = jnp.zeros((bp, POOL_KEEP, d), f32)
    pooled_p = pool_branch(u, pool0_p, row0=0, nb=bp, seq=lp, n_past=0, tl=256, name="pool_prompt")
    pooled_s = pool_branch(u, state_pool[0], row0=mp, nb=bs, seq=ls, n_past=PAST_LEN, tl=ls,
                           nseq=SAMPLE_SEQS_PER_STEP, name="pool_sample")
    pm = pool_mix(pooled_p, pooled_s, w_pool_mix[0], pool_scale[0], tm=TM, name="pool_mix")

    ya_p = matmul(yn_p, w_ssd_out[0], **dict(KT, tm=2 * TM), name="ssd_out_prompt")
    ya_s = matmul(yn_s, w_ssd_out[0], **dict(KT, tm=TM), name="ssd_out_sample")
    merged = matmul(pm, w_pool_out[0], out_dtype=bf16, name="pool_out_merge",
                    epilogue=lambda yb, ga, gb, ya_tp, ya_ts: (ga.astype(f32) * _pick_stacked(ya_tp, ya_ts, n_p)
                                                               + gb.astype(f32) * yb),
                    extras=[_tile(sg), (sg, (TM, TN), lambda i, j, k: (i, j + D_MODEL // TN))]
                    + _stacked_tiles(ya_p, ya_s))
    x1, x1g, ssq1 = matmul(merged, w_o[0], name="out_proj", norm_gain=g_ffn[0],
                           epilogue=lambda acc, r_p, r_s: _pick_stacked(r_p, r_s, n_p) + acc,
                           extras=_stacked_tiles(xp2, xs2))

    act = matmul(x1g, w_up[0], tm=TM_BIG, out_dtype=bf16, name="mlp_up",
                 epilogue=lambda acc, ssq: jnp.square(jnp.maximum(acc * _row_scale(ssq, D_MODEL, EPS), 0.0)),
                 extras=[_ssq_tiles(ssq1, TM_BIG)])
    x2, x2g, ssq2 = matmul(act, w_down[0], **KT, name="mlp_down", norm_gain=g_ple[0],
                           extras=[_tile(x1, KT["tm"], KT["tn"])])

    x3 = matmul(x2g, w_ple_gate[0], name="ple", epilogue=_ple_epilogue,
                extras=[_tile(x2),
                        (pe, (TM, D_PLE), lambda i, j, k: (i, 0)),
                        (w_ple_proj[0], (D_PLE, TN), lambda i, j, k: (0, j)),
                        _ssq_tiles(ssq2, TM)])

    y_p = rmsnorm(x3, g_final, f32, rows=mp, name="norm_final_prompt").reshape(bp, lp, d)
    y_s = rmsnorm(x3, g_final, f32, row_block0=mp // 512, rows=ms, name="norm_final_sample").reshape(bs, ls, d)

    conv_p = _last_rows(xbc, lp, CONV_K - 1, 0, bp)
    pool_p = _last_rows(u, lp, POOL_KEEP, 0, bp)
    xbc_s = xbc.reshape(-1, SUBLANES, CONV_DIM)[mp // SUBLANES:]
    u_s = u.reshape(-1, SUBLANES, d)[mp // SUBLANES:]
    conv_s = xbc_s[:, ls - (CONV_K - 1):]
    pool_s = jnp.concatenate([state_pool[0][:, ls:], u_s], axis=1)
    return (y_p, y_s, ssm_p[None], conv_p[None], pool_p[None], ssm_s[None], conv_s[None], pool_s[None])
```

```python
import functools

import jax
import jax.numpy as jnp
from jax import lax
from jax.experimental import pallas as pl
from jax.experimental.pallas import tpu as pltpu

f32 = jnp.float32
bf16 = jnp.bfloat16

D_MODEL = 4096
D_INNER = 8192
HEAD_DIM = 64
N_HEADS = 128
N_GROUPS = 8
HEADS_PER_GROUP = N_HEADS // N_GROUPS
GROUP_W = D_INNER // N_GROUPS
D_STATE = 128
CONV_K = 4
CONV_DIM = D_INNER + 2 * N_GROUPS * D_STATE
XBC_W = GROUP_W + 2 * D_STATE
POOL_WINDOWS = (2, 4, 8, 16)
POOL_GROUP = 1024
POOL_KEEP = 15
D_PLE = 256
EPS = 1e-6
SSD_NORM_EPS = 1e-5
CHUNK = 128
PAST_LEN = 16384

LANES = 128
SUBLANES = 8
VMEM_LIMIT = 56 * 1024 * 1024
TM = 1024
TM_BIG = 1536
TN = 512
KT = dict(tm=TM_BIG, tn=1024, tk=1024)
SAMPLE_SEQS_PER_STEP = 16


def _params(sem, fuse_inputs=None):
    return pltpu.CompilerParams(dimension_semantics=sem, vmem_limit_bytes=VMEM_LIMIT,
                                allow_input_fusion=fuse_inputs)


def _rms(x, g, eps):
    ms = jnp.mean(x * x, axis=-1, keepdims=True)
    return x * lax.rsqrt(ms + eps) * g


def _rmsnorm_kernel(x_ref, g_ref, o_ref, *, eps):
    o_ref[...] = _rms(x_ref[...], g_ref[...], eps).astype(o_ref.dtype)


def rmsnorm(x, g, out_dtype, *, name, tm=512, row_block0=0, rows=None):
    m, d = x.shape
    rows = m if rows is None else rows
    return pl.pallas_call(
        functools.partial(_rmsnorm_kernel, eps=EPS),
        grid=(rows // tm,),
        in_specs=[pl.BlockSpec((tm, d), lambda i: (i + row_block0, 0)),
                  pl.BlockSpec((1, d), lambda i: (0, 0))],
        out_specs=pl.BlockSpec((tm, d), lambda i: (i, 0)),
        out_shape=jax.ShapeDtypeStruct((rows, d), out_dtype),
        compiler_params=_params(("parallel",)),
        name=name,
    )(x, g.reshape(1, d))


def _rmsnorm2_kernel(xp_ref, xs_ref, g_ref, o_ref, *, eps, n_p):
    i = pl.program_id(0)

    @pl.when(i < n_p)
    def _():
        o_ref[...] = _rms(xp_ref[...], g_ref[...], eps).astype(o_ref.dtype)

    @pl.when(i >= n_p)
    def _():
        o_ref[...] = _rms(xs_ref[...], g_ref[...], eps).astype(o_ref.dtype)


def rmsnorm_stacked(x_p, x_s, g, out_dtype, *, name, tm=512):
    (mp, d), ms = x_p.shape, x_s.shape[0]
    n_p = mp // tm
    return pl.pallas_call(
        functools.partial(_rmsnorm2_kernel, eps=EPS, n_p=n_p),
        grid=((mp + ms) // tm,),
        in_specs=[pl.BlockSpec((tm, d), lambda i: (jnp.minimum(i, n_p - 1), 0)),
                  pl.BlockSpec((tm, d), lambda i: (jnp.maximum(i - n_p, 0), 0)),
                  pl.BlockSpec((1, d), lambda i: (0, 0))],
        out_specs=pl.BlockSpec((tm, d), lambda i: (i, 0)),
        out_shape=jax.ShapeDtypeStruct((mp + ms, d), out_dtype),
        compiler_params=_params(("arbitrary",)),
        name=name,
    )(x_p, x_s, g.reshape(1, d))


def _norm_parts(x, gain, hg_ref, ssq_ref):
    hg_ref[...] = (x * gain).astype(hg_ref.dtype)
    part = jnp.broadcast_to(jnp.sum(x * x, axis=-1, keepdims=True), ssq_ref.shape)
    j = pl.program_id(1)

    @pl.when(j == 0)
    def _():
        ssq_ref[...] = part

    @pl.when(j > 0)
    def _():
        ssq_ref[...] += part


def _mm_kernel(*refs, nk, n_extra, epilogue, with_norm):
    a_ref, w_ref = refs[0], refs[1]
    extras = refs[2:2 + n_extra]
    gain_ref = refs[2 + n_extra] if with_norm else None
    o_ref = refs[2 + n_extra + with_norm]
    norm_refs = refs[3 + n_extra + with_norm:]

    def product():
        return jnp.dot(a_ref[...], w_ref[...].astype(bf16), preferred_element_type=f32)

    if nk == 1:
        out = epilogue(product(), *[e[...] for e in extras])
        o_ref[...] = out.astype(o_ref.dtype)
        if with_norm:
            _norm_parts(out, gain_ref[...], *norm_refs)
    else:
        k = pl.program_id(2)

        @pl.when(k == 0)
        def _():
            o_ref[...] = extras[0][...] + product() if n_extra else product()

        @pl.when(k > 0)
        def _():
            o_ref[...] += product()

        if with_norm:
            @pl.when(k == nk - 1)
            def _():
                _norm_parts(o_ref[...], gain_ref[...], *norm_refs)


def matmul(a, w, *, name, n=None, col0=0, tm=TM, tn=TN, tk=None, out_dtype=f32, epilogue=None, extras=(),
           norm_gain=None):
    m, kdim = a.shape
    n = w.shape[1] if n is None else n
    tk = kdim if tk is None else tk
    nk = kdim // tk
    if nk > 1:
        assert epilogue is None and out_dtype == f32 and len(extras) <= 1
    if epilogue is None:
        epilogue = lambda acc: acc
    if col0 % tn == 0:
        w_spec = pl.BlockSpec((tk, tn), lambda i, j, k: (k, col0 // tn + j))
    else:
        w_spec = pl.BlockSpec((pl.Element(tk), pl.Element(tn)),
                              lambda i, j, k: (k * tk, pl.multiple_of(col0 + j * tn, LANES)))
    in_specs = [pl.BlockSpec((tm, tk), lambda i, j, k: (i, k)), w_spec]
    in_specs += [pl.BlockSpec(blk, imap) for (_, blk, imap) in extras]
    args = [a, w] + [e[0] for e in extras]
    out_specs = pl.BlockSpec((tm, tn), lambda i, j, k: (i, j))
    out_shape = jax.ShapeDtypeStruct((m, n), out_dtype)
    if norm_gain is not None:
        in_specs.append(pl.BlockSpec((1, tn), lambda i, j, k: (0, j)))
        args.append(norm_gain.reshape(1, n))
        out_specs = (out_specs, pl.BlockSpec((tm, tn), lambda i, j, k: (i, j)),
                     pl.BlockSpec((tm, LANES), lambda i, j, k: (i, 0)))
        out_shape = (out_shape, jax.ShapeDtypeStruct((m, n), bf16), jax.ShapeDtypeStruct((m, LANES), f32))
    col_sem = "parallel" if norm_gain is None else "arbitrary"
    return pl.pallas_call(
        functools.partial(_mm_kernel, nk=nk, n_extra=len(extras), epilogue=epilogue,
                          with_norm=norm_gain is not None),
        grid=(m // tm, n // tn, nk),
        in_specs=in_specs,
        out_specs=out_specs,
        out_shape=out_shape,
        compiler_params=_params(("parallel", col_sem, "arbitrary")),
        name=name,
    )(*args)


def _row_scale(ssq, dim, eps):
    return lax.rsqrt(ssq[:, 0:1] * (1.0 / dim) + eps)


def _ssq_tiles(ssq, tm):
    return (ssq, (tm, LANES), lambda i, j, k: (i, 0))


def _tile(arr, tm=TM, tn=TN):
    return (arr, (tm, tn), lambda i, j, k: (i, j))


def _stacked_tiles(x_p, x_s, tm=TM, tn=TN):
    n_p = x_p.shape[0] // tm
    return [(x_p, (tm, tn), lambda i, j, k: (jnp.minimum(i, n_p - 1), j)),
            (x_s, (tm, tn), lambda i, j, k: (jnp.maximum(i - n_p, 0), j))]


def _pick_stacked(t_p, t_s, n_p):
    return jnp.where(pl.program_id(0) < n_p, t_p, t_s)


def _silu(x):
    return x * jax.nn.sigmoid(x)


def _softplus(x):
    return jnp.maximum(x, 0.0) + jnp.log1p(jnp.exp(-jnp.abs(x)))


def _expand_heads(x, rows):
    lane = lax.broadcasted_iota(jnp.int32, (rows, LANES), 1)
    tiles = []
    for k in range(HEADS_PER_GROUP // 2):
        a = jnp.broadcast_to(x[:, 2 * k:2 * k + 1], (rows, LANES))
        b = jnp.broadcast_to(x[:, 2 * k + 1:2 * k + 2], (rows, LANES))
        tiles.append(jnp.where(lane < HEAD_DIM, a, b))
    return jnp.concatenate(tiles, axis=1)


def _conv_silu(xpad, r0, cw_refs, cb_refs, q):
    w = jnp.concatenate([r[...] for r in cw_refs], axis=1)
    bias = jnp.concatenate([r[...] for r in cb_refs], axis=1)
    x = xpad[pl.ds(r0 - SUBLANES, q + SUBLANES), :]
    acc = w[0:1, :] * x
    for k in range(1, CONV_K):
        acc = pltpu.roll(acc, 1, axis=0) + w[k:k + 1, :] * x
    return _silu(acc[SUBLANES:, :] + bias)


def _stage(xpad, r0, nrows, xs, b, c):
    xpad[pl.ds(r0, nrows), 0:GROUP_W] = xs
    xpad[pl.ds(r0, nrows), GROUP_W:GROUP_W + D_STATE] = b
    xpad[pl.ds(r0, nrows), GROUP_W + D_STATE:XBC_W] = c


def _stage_halo(xpad, r0, pxs, pb, pc):
    xpad[pl.ds(r0 - SUBLANES, SUBLANES), :] = jnp.zeros((SUBLANES, XBC_W), f32)
    _stage(xpad, r0 - (CONV_K - 1), CONV_K - 1, pxs, pb, pc)


def _gate_norm(y, xs, dsk_ref, zs_ref, gout_ref):
    y = y + dsk_ref[...] * xs
    y = y * zs_ref[...].astype(f32)
    ms = jnp.mean(y * y, axis=-1, keepdims=True)
    return y * lax.rsqrt(ms + SSD_NORM_EPS) * gout_ref[...]


def _side_projection(h_ref, w_bf16, act):
    return act(jnp.dot(h_ref[...], w_bf16[...], preferred_element_type=f32))


def _decay_rows(cd_b):
    return jnp.concatenate(
        [jnp.broadcast_to(cd_b[h:h + 1, :], (HEAD_DIM, D_STATE)) for h in range(HEADS_PER_GROUP)], axis=0)


def _ssd_prompt_kernel(xs_ref, b_ref, c_ref, pxs_ref, pb_ref, pc_ref,
                       cwx_ref, cwb_ref, cwc_ref, cbx_ref, cbb_ref, cbc_ref,
                       dt_ref, dtt_ref, dtb_ref, dtbt_ref, alog_ref, alogt_ref,
                       dsk_ref, zs_ref, gout_ref, h_ref, wg_ref,
                       yn_ref, ssm_ref, sg_ref, xpad, state, wg_bf16, *, nc, n_mt, side_act):
    q = CHUNK
    r0 = SUBLANES
    s = pl.program_id(0)
    c = s % nc
    hi = lax.Precision.HIGHEST

    @pl.when(s % n_mt == 0)
    def _():
        wg_bf16[...] = wg_ref[...].astype(bf16)

    @pl.when(c == 0)
    def _():
        state[...] = jnp.zeros_like(state)
        _stage_halo(xpad, r0, pxs_ref[0], pb_ref[0], pc_ref[0])

    @pl.when(c > 0)
    def _():
        xpad[pl.ds(r0 - 3, 3), :] = xpad[pl.ds(r0 + q - 3, 3), :]

    sg_ref[...] = _side_projection(h_ref, wg_bf16, side_act).astype(sg_ref.dtype)
    _stage(xpad, r0, q, xs_ref[...], b_ref[...], c_ref[...])
    act = _conv_silu(xpad, r0, (cwx_ref, cwb_ref, cwc_ref), (cbx_ref, cbb_ref, cbc_ref), q)
    xs = act[:, 0:GROUP_W]
    bm = act[:, GROUP_W:GROUP_W + D_STATE].astype(bf16)
    cm = act[:, GROUP_W + D_STATE:XBC_W].astype(bf16)

    dt = _softplus(dt_ref[0] + dtb_ref[0])
    dta = dt * (-jnp.exp(alog_ref[0]))
    dtt = _softplus(dtt_ref[0] + dtbt_ref[0])
    dtat = dtt * (-jnp.exp(alogt_ref[0]))
    row = lax.broadcasted_iota(jnp.int32, (q, q), 0)
    col = lax.broadcasted_iota(jnp.int32, (q, q), 1)
    causal = row >= col
    acum = jnp.dot(causal.astype(f32), dta, precision=hi, preferred_element_type=f32)
    acum_t = jnp.dot(dtat, (row <= col).astype(f32), precision=hi, preferred_element_type=f32)
    tot_b = jnp.dot(dtat, jnp.ones((q, D_STATE), f32), precision=hi, preferred_element_type=f32)

    dt_x = _expand_heads(dt, q)
    acum_x = _expand_heads(acum, q)
    xdt = xs * dt_x
    decay_x = jnp.exp(acum_x[q - 1:q, :] - acum_x)
    xdtd = (xdt * decay_x).astype(bf16)
    xdt_b = xdt.astype(bf16)

    cb = lax.dot_general(cm, bm, (((1,), (1,)), ((), ())), preferred_element_type=f32)
    cb = jnp.where(causal, cb, 0.0)
    lane = lax.broadcasted_iota(jnp.int32, (q, LANES), 1)
    tiles = []
    for k in range(HEADS_PER_GROUP // 2):
        res = []
        for h in (2 * k, 2 * k + 1):
            seg = jnp.broadcast_to(acum[:, h:h + 1], (q, q)) - jnp.broadcast_to(acum_t[h:h + 1, :], (q, q))
            m_h = (cb * jnp.exp(jnp.minimum(seg, 0.0))).astype(bf16)
            res.append(jnp.dot(m_h, xdt_b[:, k * LANES:(k + 1) * LANES], preferred_element_type=f32))
        tiles.append(jnp.where(lane < HEAD_DIM, res[0], res[1]))
    y_diag = jnp.concatenate(tiles, axis=1)

    s_prev = state[...]
    y_off = lax.dot_general(cm, s_prev.astype(bf16), (((1,), (1,)), ((), ())), preferred_element_type=f32)
    y_off = y_off * jnp.exp(acum_x)
    new_states = lax.dot_general(xdtd, bm, (((0,), (0,)), ((), ())), preferred_element_type=f32)
    s_new = s_prev * _decay_rows(jnp.exp(tot_b)) + new_states
    state[...] = s_new
    yn_ref[...] = _gate_norm(y_diag + y_off, xs, dsk_ref, zs_ref, gout_ref).astype(yn_ref.dtype)

    @pl.when(c == nc - 1)
    def _():
        ssm_ref[0, 0] = state[...]


def _ssd_short_chunk(xpad, r0, cw_refs, cb_refs, dt_raw, dtb, alog, s_prev, q):
    hi = lax.Precision.HIGHEST
    act = _conv_silu(xpad, r0, cw_refs, cb_refs, q)
    xs = act[:, 0:GROUP_W]
    bm = act[:, GROUP_W:GROUP_W + D_STATE].astype(bf16)
    cm = act[:, GROUP_W + D_STATE:XBC_W].astype(bf16)

    dt = _softplus(dt_raw + dtb)
    dta = dt * (-jnp.exp(alog))
    row = lax.broadcasted_iota(jnp.int32, (q, q), 0)
    col = lax.broadcasted_iota(jnp.int32, (q, q), 1)
    acum = jnp.dot((row >= col).astype(f32), dta, precision=hi, preferred_element_type=f32)
    tot_b = lax.dot_general(dta, jnp.ones((q, D_STATE), f32), (((0,), (0,)), ((), ())),
                            precision=hi, preferred_element_type=f32)

    dt_x = _expand_heads(dt, q)
    acum_x = _expand_heads(acum, q)
    xdt = xs * dt_x
    decay_x = jnp.exp(acum_x[q - 1:q, :] - acum_x)
    xdtd = (xdt * decay_x).astype(bf16)

    cb = lax.dot_general(cm, bm, (((1,), (1,)), ((), ())), preferred_element_type=f32)
    rowx = lax.broadcasted_iota(jnp.int32, (q, GROUP_W), 0)
    y = jnp.zeros((q, GROUP_W), f32)
    for j in range(q):
        wj = jnp.where(rowx >= j, jnp.exp(acum_x - acum_x[j:j + 1, :]), 0.0)
        wj = wj * jnp.broadcast_to(cb[:, j:j + 1], (q, GROUP_W))
        y = y + wj * xdt[j:j + 1, :]

    y_off = lax.dot_general(cm, s_prev.astype(bf16), (((1,), (1,)), ((), ())), preferred_element_type=f32)
    y = y + y_off * jnp.exp(acum_x)
    new_states = lax.dot_general(xdtd, bm, (((0,), (0,)), ((), ())), preferred_element_type=f32)
    return y, xs, s_prev * _decay_rows(jnp.exp(tot_b)) + new_states


def _ssd_sample_kernel(xs_ref, b_ref, c_ref, pxs_ref, pb_ref, pc_ref,
                       cwx_ref, cwb_ref, cwc_ref, cbx_ref, cbb_ref, cbc_ref,
                       dt_ref, dtb_ref, alog_ref, dsk_ref, zs_ref, gout_ref, s0_ref,
                       yn_ref, ssm_ref, xpad, *, q, nseq):
    ys, xss, states = [], [], []
    stride = 2 * SUBLANES
    for n in range(nseq):
        r0 = n * stride + SUBLANES
        rows = pl.ds(n * q, q)
        _stage_halo(xpad, r0, pxs_ref[n], pb_ref[n], pc_ref[n])
        _stage(xpad, r0, q, xs_ref[rows, :], b_ref[rows, :], c_ref[rows, :])
        y, xs, s_new = _ssd_short_chunk(xpad, r0, (cwx_ref, cwb_ref, cwc_ref), (cbx_ref, cbb_ref, cbc_ref),
                                        dt_ref[0, rows, :], dtb_ref[0], alog_ref[0], s0_ref[n, 0], q)
        states.append(s_new)
        ys.append(y)
        xss.append(xs)
    yn = _gate_norm(jnp.concatenate(ys, axis=0), jnp.concatenate(xss, axis=0), dsk_ref, zs_ref, gout_ref)
    for n in range(nseq):
        ssm_ref[n, 0] = states[n]
    yn_ref[...] = yn.astype(yn_ref.dtype)


def _group_param(v):
    return v.reshape(N_GROUPS, 1, HEADS_PER_GROUP), v.reshape(N_GROUPS, HEADS_PER_GROUP, 1)


def ssd_branch(xbc, dt_raw, zs, conv_prev, s0, conv_w, conv_b, dt_bias, a_log, d_skip, g_out,
               *, row0, nb, seq, name, side_proj=None):
    m = xbc.shape[0]
    q = min(seq, CHUNK)
    nc = seq // q
    nseq = 1 if s0 is None else SAMPLE_SEQS_PER_STEP
    rows = nseq * q
    rb0 = row0 // rows
    dtg = dt_raw.reshape(m, N_GROUPS, HEADS_PER_GROUP).transpose(1, 0, 2)
    dtb, dtbt = _group_param(dt_bias)
    alog, alogt = _group_param(a_log)
    dsk = jnp.repeat(d_skip, HEAD_DIM).reshape(1, D_INNER)
    gout = g_out.reshape(1, D_INNER)
    cb2 = conv_b.reshape(1, CONV_DIM)
    nbg, ncg = D_INNER // D_STATE, (D_INNER + N_GROUPS * D_STATE) // D_STATE

    if s0 is None:
        steps = nb * N_GROUPS * nc
        ix = lambda f: (lambda s: f(s // (N_GROUPS * nc), (s // nc) % N_GROUPS,
                                    rb0 + (s // (N_GROUPS * nc)) * nc + s % nc))
    else:
        assert nc == 1 and nb % nseq == 0
        steps = (nb // nseq) * N_GROUPS
        ix = lambda f: (lambda s: f(s // N_GROUPS, s % N_GROUPS, rb0 + s // N_GROUPS))
    grid = (steps,)
    sem = ("arbitrary",)

    in_specs = [
        pl.BlockSpec((rows, GROUP_W), ix(lambda b, g, r: (r, g))),
        pl.BlockSpec((rows, D_STATE), ix(lambda b, g, r: (r, nbg + g))),
        pl.BlockSpec((rows, D_STATE), ix(lambda b, g, r: (r, ncg + g))),
        pl.BlockSpec((nseq, CONV_K - 1, GROUP_W), ix(lambda b, g, r: (b, 0, g))),
        pl.BlockSpec((nseq, CONV_K - 1, D_STATE), ix(lambda b, g, r: (b, 0, nbg + g))),
        pl.BlockSpec((nseq, CONV_K - 1, D_STATE), ix(lambda b, g, r: (b, 0, ncg + g))),
        pl.BlockSpec((CONV_K, GROUP_W), ix(lambda b, g, r: (0, g))),
        pl.BlockSpec((CONV_K, D_STATE), ix(lambda b, g, r: (0, nbg + g))),
        pl.BlockSpec((CONV_K, D_STATE), ix(lambda b, g, r: (0, ncg + g))),
        pl.BlockSpec((1, GROUP_W), ix(lambda b, g, r: (0, g))),
        pl.BlockSpec((1, D_STATE), ix(lambda b, g, r: (0, nbg + g))),
        pl.BlockSpec((1, D_STATE), ix(lambda b, g, r: (0, ncg + g))),
        pl.BlockSpec((1, rows, HEADS_PER_GROUP), ix(lambda b, g, r: (g, r, 0))),
    ]
    args = [xbc, xbc, xbc, conv_prev, conv_prev, conv_prev, conv_w, conv_w, conv_w, cb2, cb2, cb2, dtg]
    vec16 = pl.BlockSpec((1, 1, HEADS_PER_GROUP), ix(lambda b, g, r: (g, 0, 0)))
    vec16t = pl.BlockSpec((1, HEADS_PER_GROUP, 1), ix(lambda b, g, r: (g, 0, 0)))
    tail_specs = [
        pl.BlockSpec((1, GROUP_W), ix(lambda b, g, r: (0, g))),
        pl.BlockSpec((rows, GROUP_W), ix(lambda b, g, r: (r, g))),
        pl.BlockSpec((1, GROUP_W), ix(lambda b, g, r: (0, g))),
    ]
    state_spec = pl.BlockSpec((nseq, 1, GROUP_W, D_STATE), ix(lambda b, g, r: (b, g, 0, 0)))
    yn_spec = pl.BlockSpec((rows, GROUP_W), ix(lambda b, g, r: (r - rb0, g)))
    out_shape = (jax.ShapeDtypeStruct((nb * seq, D_INNER), bf16),
                 jax.ShapeDtypeStruct((nb, N_GROUPS, GROUP_W, D_STATE), f32))

    if s0 is None:
        dtgt = dt_raw.reshape(m, N_GROUPS, HEADS_PER_GROUP).transpose(1, 2, 0)
        in_specs += [pl.BlockSpec((1, HEADS_PER_GROUP, q), ix(lambda b, g, r: (g, 0, r))),
                     vec16, vec16t, vec16, vec16t] + tail_specs
        args += [dtgt, dtb, dtbt, alog, alogt, dsk, zs, gout]
        ph, pw, pcol0, pn, pact, pdtype = side_proj
        ptn = 2 * LANES
        n_nt = pn // ptn
        n_mt = steps // n_nt
        ptm = ph.shape[0] // n_mt
        assert n_mt * n_nt == steps and ptm * n_mt == ph.shape[0] and ptm % (2 * SUBLANES) == 0
        pk = ph.shape[1]
        in_specs += [pl.BlockSpec((ptm, pk), lambda s: (s % n_mt, 0)),
                     pl.BlockSpec((pl.Element(pk), pl.Element(ptn)),
                                  lambda s: (0, pl.multiple_of(pcol0 + (s // n_mt) * ptn, LANES)))]
        args += [ph, pw]
        out_specs = (yn_spec, state_spec, pl.BlockSpec((ptm, ptn), lambda s: (s % n_mt, s // n_mt)))
        out_shape = out_shape + (jax.ShapeDtypeStruct((ph.shape[0], pn), pdtype),)
        kern = functools.partial(_ssd_prompt_kernel, nc=nc, n_mt=n_mt, side_act=pact)
        scratch = [pltpu.VMEM((SUBLANES + q, XBC_W), f32), pltpu.VMEM((GROUP_W, D_STATE), f32),
                   pltpu.VMEM((pk, ptn), bf16)]
    else:
        in_specs += [vec16, vec16] + tail_specs + [state_spec]
        args += [dtb, alog, dsk, zs, gout, s0.reshape(nb, N_GROUPS, GROUP_W, D_STATE)]
        out_specs = (yn_spec, state_spec)
        kern = functools.partial(_ssd_sample_kernel, q=q, nseq=nseq)
        scratch = [pltpu.VMEM((nseq * 2 * SUBLANES, XBC_W), f32)]

    outs = pl.pallas_call(
        kern, grid=grid, in_specs=in_specs, out_specs=out_specs, out_shape=out_shape,
        scratch_shapes=scratch, name=name,
        compiler_params=_params(sem, [a is dtg or (s0 is None and a is dtgt) for a in args]),
    )(*args)
    return (outs[0], outs[1].reshape(nb, N_HEADS, HEAD_DIM, D_STATE)) + tuple(outs[2:])


def _pool_kernel(u_ref, prev_ref, o_ref, halo, *, tl, n_past, nseq, carry):
    lt = pl.program_id(1)
    base = 2 * SUBLANES
    stride = base + tl

    def init_halo(n):
        halo[pl.ds(n * stride, base - POOL_KEEP), :] = jnp.zeros((base - POOL_KEEP, halo.shape[1]), f32)
        halo[pl.ds(n * stride + base - POOL_KEEP, POOL_KEEP), :] = prev_ref[n]

    if carry:
        assert nseq == 1
        pl.when(lt == 0)(lambda: init_halo(0))

        @pl.when(lt > 0)
        def _():
            halo[pl.ds(base - POOL_KEEP, POOL_KEEP), :] = halo[pl.ds(base + tl - POOL_KEEP, POOL_KEEP), :]
    else:
        for n in range(nseq):
            init_halo(n)

    for n in range(nseq):
        halo[pl.ds(n * stride + base, tl), :] = u_ref[pl.ds(n * tl, tl), :]
    t = lt * tl + lax.broadcasted_iota(jnp.int32, (tl, 1), 0) + (n_past + 1)
    for gi, w in enumerate(POOL_WINDOWS):
        cols = slice(gi * POOL_GROUP, (gi + 1) * POOL_GROUP)
        cnt = jnp.minimum(t, w).astype(f32)
        outs = []
        for n in range(nseq):
            x = halo[pl.ds(n * stride, stride), cols]
            acc, shift = x, 1
            while shift < w:
                acc = acc + pltpu.roll(acc, shift, axis=0)
                shift *= 2
            outs.append(acc[base:, :] / cnt - x[base:, :])
        o_ref[:, cols] = jnp.concatenate(outs, axis=0).astype(o_ref.dtype)


def pool_branch(u, prev, *, row0, nb, seq, n_past, tl, name, nseq=1):
    d = u.shape[1]
    nlt = seq // tl
    assert nseq == 1 or nlt == 1
    rows = nseq * tl
    rb0 = row0 // rows
    return pl.pallas_call(
        functools.partial(_pool_kernel, tl=tl, n_past=n_past, nseq=nseq, carry=nlt > 1),
        grid=(nb // nseq, nlt),
        in_specs=[pl.BlockSpec((rows, d), lambda b, l: (rb0 + b * nlt + l, 0)),
                  pl.BlockSpec((nseq, POOL_KEEP, d), lambda b, l: (b, 0, 0))],
        out_specs=pl.BlockSpec((rows, d), lambda b, l: (b * nlt + l, 0)),
        out_shape=jax.ShapeDtypeStruct((nb * seq, d), bf16),
        scratch_shapes=[pltpu.VMEM((nseq * (2 * SUBLANES + tl), d), f32)],
        compiler_params=_params(("parallel", "arbitrary")),
        name=name,
    )(u, prev)


def _poolmix_kernel(ap_ref, as_ref, w_ref, s_ref, o_ref, *, n_p):
    def mix(a_ref):
        acc = jnp.dot(a_ref[...], w_ref[0].astype(bf16), preferred_element_type=f32)
        o_ref[...] = (acc * s_ref[...]).astype(o_ref.dtype)

    i = pl.program_id(0)
    pl.when(i < n_p)(lambda: mix(ap_ref))
    pl.when(i >= n_p)(lambda: mix(as_ref))


def pool_mix(pooled_p, pooled_s, w_mix, scale, *, tm, name):
    (mp, d), ms = pooled_p.shape, pooled_s.shape[0]
    n_p = mp // tm
    ng = d // POOL_GROUP
    return pl.pallas_call(
        functools.partial(_poolmix_kernel, n_p=n_p),
        grid=((mp + ms) // tm, ng),
        in_specs=[pl.BlockSpec((tm, POOL_GROUP), lambda i, g: (jnp.minimum(i, n_p - 1), g)),
                  pl.BlockSpec((tm, POOL_GROUP), lambda i, g: (jnp.maximum(i - n_p, 0), g)),
                  pl.BlockSpec((1, POOL_GROUP, POOL_GROUP), lambda i, g: (g, 0, 0)),
                  pl.BlockSpec((1, POOL_GROUP), lambda i, g: (0, g))],
        out_specs=pl.BlockSpec((tm, POOL_GROUP), lambda i, g: (i, g)),
        out_shape=jax.ShapeDtypeStruct((mp + ms, d), bf16),
        compiler_params=_params(("arbitrary", "arbitrary")),
        name=name,
    )(pooled_p, pooled_s, w_mix, scale.reshape(1, d))


def _ple_epilogue(acc, x2, p, wp, ssq):
    pe = jnp.dot(p, wp.astype(bf16), preferred_element_type=f32)
    return x2 + jax.nn.sigmoid(acc * _row_scale(ssq, D_MODEL, EPS)) * pe


def _last_rows(a2d, seq, keep, row0, nb):
    grp = 2 * SUBLANES
    a3 = a2d.reshape(a2d.shape[0] // grp, grp, a2d.shape[1])
    first = row0 // grp + seq // grp - 1
    last16 = lax.slice(a3, (first, 0, 0), (first + (nb - 1) * (seq // grp) + 1, grp, a2d.shape[1]),
                       (seq // grp, 1, 1))
    return last16[:, grp - keep:]


def kernel(x_prompt, x_sample, p_prompt, p_sample, state_ssm, state_conv, state_pool, g_mix, w_in, conv_w, conv_b, dt_bias, a_log, d_skip, g_ssd_out, w_ssd_out, w_pool_mix, pool_scale, w_pool_out, w_o, g_ffn, w_up, w_down, g_ple, w_ple_gate, w_ple_proj, g_final):
    bp, lp, d = x_prompt.shape
    bs, ls, _ = x_sample.shape
    mp, ms = bp * lp, bs * ls
    n_p = mp // TM
    xp2, xs2 = x_prompt.reshape(mp, d), x_sample.reshape(ms, d)
    pe = jnp.concatenate([p_prompt[0].reshape(mp, D_PLE), p_sample[0].reshape(ms, D_PLE)], axis=0).astype(bf16)

    c_xbc = D_INNER
    c_dt = c_xbc + CONV_DIM
    c_u = c_dt + N_HEADS
    c_gate = c_u + D_MODEL
    w_in0 = w_in[0]

    h = rmsnorm_stacked(xp2, xs2, g_mix[0], bf16, name="norm_mix")
    zs = matmul(h, w_in0, n=D_INNER, col0=0, tm=TM_BIG, out_dtype=bf16, epilogue=_silu, name="proj_z")
    xbc = matmul(h, w_in0, n=CONV_DIM, col0=c_xbc, tm=TM_BIG, name="proj_xbc")
    dt_raw = matmul(h, w_in0, n=N_HEADS, col0=c_dt, tm=TM_BIG, tn=N_HEADS, name="proj_dt")
    u = matmul(h, w_in0, n=D_MODEL, col0=c_u, tm=TM_BIG, name="proj_u")

    ssd_w = (conv_w[0], conv_b[0], dt_bias[0], a_log[0], d_skip[0], g_ssd_out[0])
    conv0_p = jnp.zeros((bp, CONV_K - 1, CONV_DIM), f32)
    yn_p, ssm_p, sg = ssd_branch(xbc, dt_raw, zs, conv0_p, None, *ssd_w, row0=0, nb=bp, seq=lp,
                                 name="ssd_prompt_gates",
                                 side_proj=(h, w_in0, c_gate, 2 * D_MODEL, jax.nn.sigmoid, bf16))
    yn_s, ssm_s = ssd_branch(xbc, dt_raw, zs, state_conv[0], state_ssm[0], *ssd_w, row0=mp, nb=bs, seq=ls,
                             name="ssd_sample")

    pool0_p = jnp.zeros((bp, POOL_KEEP, d), f32)
    pooled_p = pool_branch(u, pool0_p, row0=0, nb=bp, seq=lp, n_past=0, tl=256, name="pool_prompt")
    pooled_s = pool_branch(u, state_pool[0], row0=mp, nb=bs, seq=ls, n_past=PAST_LEN, tl=ls,
                           nseq=SAMPLE_SEQS_PER_STEP, name="pool_sample")
    pm = pool_mix(pooled_p, pooled_s, w_pool_mix[0], pool_scale[0], tm=TM, name="pool_mix")

    ya_p = matmul(yn_p, w_ssd_out[0], **dict(KT, tm=2 * TM), name="ssd_out_prompt")
    ya_s = matmul(yn_s, w_ssd_out[0], **dict(KT, tm=TM), name="ssd_out_sample")
    merged = matmul(pm, w_pool_out[0], out_dtype=bf16, name="pool_out_merge",
                    epilogue=lambda yb, ga, gb, ya_tp, ya_ts: (ga.astype(f32) * _pick_stacked(ya_tp, ya_ts, n_p)
                                                               + gb.astype(f32) * yb),
                    extras=[_tile(sg), (sg, (TM, TN), lambda i, j, k: (i, j + D_MODEL // TN))]
                    + _stacked_tiles(ya_p, ya_s))
    x1, x1g, ssq1 = matmul(merged, w_o[0], name="out_proj", norm_gain=g_ffn[0],
                           epilogue=lambda acc, r_p, r_s: _pick_stacked(r_p, r_s, n_p) + acc,
                           extras=_stacked_tiles(xp2, xs2))

    act = matmul(x1g, w_up[0], tm=TM_BIG, out_dtype=bf16, name="mlp_up",
                 epilogue=lambda acc, ssq: jnp.square(jnp.maximum(acc * _row_scale(ssq, D_MODEL, EPS), 0.0)),
                 extras=[_ssq_tiles(ssq1, TM_BIG)])
    x2, x2g, ssq2 = matmul(act, w_down[0], **KT, name="mlp_down", norm_gain=g_ple[0],
                           extras=[_tile(x1, KT["tm"], KT["tn"])])

    x3 = matmul(x2g, w_ple_gate[0], name="ple", epilogue=_ple_epilogue,
                extras=[_tile(x2),
                        (pe, (TM, D_PLE), lambda i, j, k: (i, 0)),
                        (w_ple_proj[0], (D_PLE, TN), lambda i, j, k: (0, j)),
                        _ssq_tiles(ssq2, TM)])

    y_p = rmsnorm(x3, g_final, f32, rows=mp, name="norm_final_prompt").reshape(bp, lp, d)
    y_s = rmsnorm(x3, g_final, f32, row_block0=mp // 512, rows=ms, name="norm_final_sample").reshape(bs, ls, d)

    conv_p = _last_rows(xbc, lp, CONV_K - 1, 0, bp)
    pool_p = _last_rows(u, lp, POOL_KEEP, 0, bp)
    xbc_s = xbc.reshape(-1, SUBLANES, CONV_DIM)[mp // SUBLANES:]
    u_s = u.reshape(-1, SUBLANES, d)[mp // SUBLANES:]
    conv_s = xbc_s[:, ls - (CONV_K - 1):]
    pool_s = jnp.concatenate([state_pool[0][:, ls:], u_s], axis=1)
    return (y_p, y_s, ssm_p[None], conv_p[None], pool_p[None], ssm_s[None], conv_s[None], pool_s[None])
```
